```python
import math
import jax
import jax.numpy as jnp
from jax import lax
import numpy as np

D_MODEL = 1024
BATCH = 2
SEQ = 8192
DEPTH = 2
DEC_BATCH = 128
DEC_SEQ = 4
PAST_LEN = 16384
PAGE_SIZE = 128

N_ATT_LAYERS = (DEPTH + 1) // 2
N_RET_LAYERS = DEPTH // 2
D_FF = 2816
RMS_EPS = 1e-6
NEG_INF = -1e30
Q_BLOCK = 128
ROPE_THETA = 500000.0
H_A = 4
HD_A = 64
ROT_A = HD_A // 4
H_B = 8
Q_LORA = 256
KV_LORA = 128
NOPE_B = 64
ROPE_B = 32
V_B = 64
H_R = 4
DK_R = 256
DV_R = 512
RET_CHUNK = 128
RET_THETA = 10000.0
ATT_SIZES = (H_A * 2 * HD_A, H_A * 2 * HD_A, H_A * 2 * HD_A, Q_LORA, KV_LORA, ROPE_B)
ATT_IN = 3 * H_A * 2 * HD_A + Q_LORA + KV_LORA + ROPE_B
ATT_MIX = H_A * 2 * HD_A + H_B * V_B
RET_SIZES = (H_R * DK_R, H_R * DK_R, H_R * DV_R, H_R * DV_R)
RET_IN = 2 * H_R * DK_R + 2 * H_R * DV_R
RET_MIX = H_R * DV_R

kernel_name = 'hybrid_diffattn_mla_retnet_macaron_step'

F32 = jnp.float32


def _split(z, sizes):
    return jnp.split(z, np.cumsum(sizes)[:-1].tolist(), axis=-1)


def _rms_norm(x, g):
    xf = x.astype(F32)
    y = xf * lax.rsqrt(jnp.mean(xf * xf, axis=-1, keepdims=True) + RMS_EPS)
    return (y * g.astype(F32)).astype(x.dtype)


def _ffn_half(x, g, wg, wu, wd):
    h = _rms_norm(x, g)
    return x + 0.5 * ((jax.nn.silu(h @ wg) * (h @ wu)) @ wd)


def _rope_tables(pos, dim, theta):
    inv = 1.0 / (theta ** (jnp.arange(0, dim, 2, dtype=F32) / dim))
    ang = pos.astype(F32)[:, None] * inv[None, :]
    return jnp.cos(ang), jnp.sin(ang)


def _rope(x, cos, sin):
    shape = (cos.shape[0],) + (1,) * (x.ndim - 3) + (cos.shape[1],)
    c = cos.reshape(shape)
    s = sin.reshape(shape)
    x1, x2 = jnp.split(x.astype(F32), 2, axis=-1)
    return jnp.concatenate([x1 * c - x2 * s, x1 * s + x2 * c], axis=-1).astype(x.dtype)


def _online_update(carry, s, v, eq):
    m, l, acc = carry
    m_new = jnp.maximum(m, jnp.max(s, axis=-1))
    corr = jnp.exp(m - m_new)
    p = jnp.exp(s - m_new[..., None])
    return (m_new, l * corr + jnp.sum(p, axis=-1),
            acc * corr[..., None] + jnp.einsum(eq, p, v.astype(F32)))


def _att_project(h, pos, w_in, q_norm_g, w_uq, kv_norm_g):
    b, t, _ = h.shape
    qa, ka, va, cq, ckv, kr = _split(h @ w_in, ATT_SIZES)
    cos_a, sin_a = _rope_tables(pos, ROT_A, ROPE_THETA)

    def partial_rope(z):
        z = z.reshape(b, t, H_A, 2, HD_A)
        return jnp.concatenate([_rope(z[..., :ROT_A], cos_a, sin_a), z[..., ROT_A:]], axis=-1)

    qa = partial_rope(qa)
    ka = partial_rope(ka).reshape(b, t, H_A, 2 * HD_A)
    va = va.reshape(b, t, H_A, 2 * HD_A)
    qb = (_rms_norm(cq, q_norm_g) @ w_uq).reshape(b, t, H_B, NOPE_B + ROPE_B)
    cos_b, sin_b = _rope_tables(pos, ROPE_B, ROPE_THETA)
    q_nope = qb[..., :NOPE_B]
    q_rope = _rope(qb[..., NOPE_B:], cos_b, sin_b)
    c = _rms_norm(ckv, kv_norm_g)
    kr = _rope(kr[:, :, None, :], cos_b, sin_b)[:, :, 0]
    return qa, ka, va, q_nope, q_rope, c, kr


def _diff_attn_prompt(qa, ka, va, lam):
    b, s = qa.shape[:2]
    nblk = s // Q_BLOCK
    scale = HD_A ** -0.5
    k = ka.reshape(b, s, H_A, 2, HD_A)
    v = va.astype(F32)
    q_blocks = qa.reshape(b, nblk, Q_BLOCK, H_A, 2, HD_A).transpose(1, 0, 2, 3, 4, 5)
    kpos = jnp.arange(s)

    def one_block(args):
        qb, i = args
        qpos = i * Q_BLOCK + jnp.arange(Q_BLOCK)
        sc = jnp.einsum('bqhmd,bkhmd->bmhqk', qb, k).astype(F32) * scale
        sc = jnp.where(kpos[None, :] <= qpos[:, None], sc, NEG_INF)
        p = jax.nn.softmax(sc, axis=-1)
        o = jnp.einsum('bmhqk,bkhe->bmqhe', p, v)
        return o[:, 0] - lam * o[:, 1]

    o = lax.map(one_block, (q_blocks, jnp.arange(nblk)))
    return o.transpose(1, 0, 2, 3, 4).reshape(b, s, H_A, 2 * HD_A)


def _diff_attn_sample(qa, ka, va, lam, cache_k, cache_v, layer, page_table):
    b, t = qa.shape[:2]
    scale = HD_A ** -0.5
    q = qa.astype(F32)
    carry = (jnp.full((b, 2, H_A, t), NEG_INF, F32), jnp.zeros((b, 2, H_A, t), F32),
             jnp.zeros((b, 2, H_A, t, 2 * HD_A), F32))

    def page_step(carry, phys):
        kp = cache_k[layer, phys].reshape(b, PAGE_SIZE, H_A, 2, HD_A)
        vp = cache_v[layer, phys]
        sc = jnp.einsum('bthmd,bphmd->bmhtp', q, kp.astype(F32)) * scale
        return _online_update(carry, sc, vp, 'bmhtp,bphe->bmhte'), None

    carry, _ = lax.scan(page_step, carry, page_table.T)
    sc = jnp.einsum('bthmd,bshmd->bmhts', q, ka.reshape(b, t, H_A, 2, HD_A).astype(F32)) * scale
    sc = jnp.where(jnp.tril(jnp.ones((t, t), bool)), sc, NEG_INF)
    _, l, acc = _online_update(carry, sc, va, 'bmhts,bshe->bmhte')
    o = acc / l[..., None]
    return (o[:, 0] - lam * o[:, 1]).transpose(0, 2, 1, 3)


def _mla_prompt(q_nope, q_rope, c, kr, w_uk, w_uv):
    b, s = c.shape[:2]
    nblk = s // Q_BLOCK
    scale = (NOPE_B + ROPE_B) ** -0.5
    k_nope = jnp.einsum('bkr,rhn->bkhn', c, w_uk)
    v = jnp.einsum('bkr,rhv->bkhv', c, w_uv).astype(F32)
    qn_blocks = q_nope.reshape(b, nblk, Q_BLOCK, H_B, NOPE_B).transpose(1, 0, 2, 3, 4)
    qr_blocks = q_rope.reshape(b, nblk, Q_BLOCK, H_B, ROPE_B).transpose(1, 0, 2, 3, 4)
    kpos = jnp.arange(s)

    def one_block(args):
        qn, qr, i = args
        qpos = i * Q_BLOCK + jnp.arange(Q_BLOCK)
        sc = (jnp.einsum('bqhn,bkhn->bhqk', qn, k_nope)
              + jnp.einsum('bqhe,bke->bhqk', qr, kr)).astype(F32) * scale
        sc = jnp.where(kpos[None, :] <= qpos[:, None], sc, NEG_INF)
        p = jax.nn.softmax(sc, axis=-1)
        return jnp.einsum('bhqk,bkhv->bqhv', p, v)

    o = lax.map(one_block, (qn_blocks, qr_blocks, jnp.arange(nblk)))
    return o.transpose(1, 0, 2, 3, 4).reshape(b, s, H_B, V_B)


def _mla_sample(q_nope, q_rope, c, kr, w_uk, w_uv, cache_lat, cache_rope, layer, page_table):
    b, t = c.shape[:2]
    scale = (NOPE_B + ROPE_B) ** -0.5
    q_lat = jnp.einsum('bthn,rhn->bhtr', q_nope, w_uk).astype(F32)
    q_r = q_rope.astype(F32).transpose(0, 2, 1, 3)

    def scores(cl, kl):
        return (jnp.einsum('bhtr,bpr->bhtp', q_lat, cl.astype(F32))
                + jnp.einsum('bhte,bpe->bhtp', q_r, kl.astype(F32))) * scale

    carry = (jnp.full((b, H_B, t), NEG_INF, F32), jnp.zeros((b, H_B, t), F32),
             jnp.zeros((b, H_B, t, KV_LORA), F32))

    def page_step(carry, phys):
        cl = cache_lat[layer, phys]
        kl = cache_rope[layer, phys]
        return _online_update(carry, scores(cl, kl), cl, 'bhtp,bpr->bhtr'), None

    carry, _ = lax.scan(page_step, carry, page_table.T)
    sc = jnp.where(jnp.tril(jnp.ones((t, t), bool)), scores(c, kr), NEG_INF)
    _, l, acc = _online_update(carry, sc, c, 'bhts,bsr->bhtr')
    o_lat = acc / l[..., None]
    return jnp.einsum('bhtr,rhv->bthv', o_lat, w_uv.astype(F32))


def _att_merge(o_a, o_b, subln_g, lam_init, w_out):
    b, t = o_a.shape[:2]
    o_a = _rms_norm(o_a, subln_g) * (1.0 - lam_init)
    mixed = jnp.concatenate([o_a.reshape(b, t, -1), o_b.reshape(b, t, -1)], axis=-1)
    return mixed.astype(w_out.dtype) @ w_out


def _ret_project(h, pos, w_in):
    b, t, _ = h.shape
    q, k, v, g = _split(h @ w_in, RET_SIZES)
    cos, sin = _rope_tables(pos, DK_R, RET_THETA)
    q = _rope(q.reshape(b, t, H_R, DK_R), cos, sin)
    k = _rope(k.reshape(b, t, H_R, DK_R), cos, sin) * (DK_R ** -0.5)
    return q, k, v.reshape(b, t, H_R, DV_R), g


def _retention(q, k, v, s0):
    b, t = q.shape[:2]
    c = RET_CHUNK if t % RET_CHUNK == 0 else t
    n = t // c
    log_g = jnp.log1p(-jnp.exp2(-5.0 - jnp.arange(H_R, dtype=F32)))
    idx = jnp.arange(c, dtype=F32)
    diff = idx[:, None] - idx[None, :]
    intra = jnp.where(diff >= 0, jnp.exp(jnp.maximum(diff, 0.0)[None] * log_g[:, None, None]), 0.0)
    q_dec = jnp.exp((idx + 1.0)[None] * log_g[:, None])
    k_dec = jnp.exp((c - 1.0 - idx)[None] * log_g[:, None])
    c_dec = jnp.exp(c * log_g)

    def chunks(z):
        return z.astype(F32).reshape(b, n, c, H_R, -1).transpose(1, 0, 3, 2, 4)

    def step(state, xs):
        qc, kc, vc = xs
        att = jnp.einsum('bhqd,bhkd->bhqk', qc, kc) * intra
        o = (jnp.einsum('bhqk,bhkv->bhqv', att, vc)
             + jnp.einsum('bhqd,bhdv->bhqv', qc * q_dec[..., None], state))
        state = state * c_dec[:, None, None] + jnp.einsum('bhkd,bhkv->bhdv', kc * k_dec[..., None], vc)
        return state, o

    s_fin, o = lax.scan(step, s0.astype(F32), (chunks(q), chunks(k), chunks(v)))
    return o.transpose(1, 0, 3, 2, 4).reshape(b, t, H_R, DV_R), s_fin


def _ret_merge(o, g, gn_g, w_out):
    b, t = o.shape[:2]
    y = _rms_norm(o, gn_g).reshape(b, t, RET_MIX) * jax.nn.silu(g.astype(F32))
    return y.astype(w_out.dtype) @ w_out


def setup_inputs(seed: int = 0) -> dict:
    key = jax.random.key(seed)
    ks = iter(jax.random.split(key, 32))

    def w(shape, fan_in):
        return jax.random.normal(next(ks), shape, F32) * (fan_in ** -0.5)

    def gain(shape):
        return 1.0 + 0.02 * jax.random.normal(next(ks), shape, F32)

    n_pages = PAST_LEN // PAGE_SIZE
    used = DEC_BATCH * n_pages
    pool = used + used // 4
    x_prompt = jax.random.normal(next(ks), (BATCH, SEQ, D_MODEL), F32)
    x_sample = jax.random.normal(next(ks), (DEC_BATCH, DEC_SEQ, D_MODEL), F32)
    cache_a_k = jax.random.normal(next(ks), (N_ATT_LAYERS, pool, PAGE_SIZE, H_A, 2 * HD_A), F32)
    cache_a_v = jax.random.normal(next(ks), (N_ATT_LAYERS, pool, PAGE_SIZE, H_A, 2 * HD_A), F32)
    cache_b_lat = jax.random.normal(next(ks), (N_ATT_LAYERS, pool, PAGE_SIZE, KV_LORA), F32)
    cache_b_rope = jax.random.normal(next(ks), (N_ATT_LAYERS, pool, PAGE_SIZE, ROPE_B), F32)
    state_ret = 0.25 * jax.random.normal(next(ks), (N_RET_LAYERS, DEC_BATCH, H_R, DK_R, DV_R), F32)
    perm = jax.random.permutation(next(ks), pool)
    page_table = perm[:used].reshape(DEC_BATCH, n_pages).astype(jnp.int32)
    return {
        'x_prompt': x_prompt,
        'x_sample': x_sample,
        'cache_a_k': cache_a_k,
        'cache_a_v': cache_a_v,
        'cache_b_lat': cache_b_lat,
        'cache_b_rope': cache_b_rope,
        'state_ret': state_ret,
        'page_table': page_table,
        'norm_g': gain((DEPTH, 3, D_MODEL)),
        'ffn_w_gate': w((DEPTH, 2, D_MODEL, D_FF), D_MODEL),
        'ffn_w_up': w((DEPTH, 2, D_MODEL, D_FF), D_MODEL),
        'ffn_w_down': w((DEPTH, 2, D_FF, D_MODEL), D_FF),
        'att_w_in': w((N_ATT_LAYERS, D_MODEL, ATT_IN), D_MODEL),
        'diff_lambda': 0.1 * jax.random.normal(next(ks), (N_ATT_LAYERS, 4, HD_A), F32),
        'diff_subln_g': gain((N_ATT_LAYERS, 2 * HD_A)),
        'mla_q_norm_g': gain((N_ATT_LAYERS, Q_LORA)),
        'mla_w_uq': w((N_ATT_LAYERS, Q_LORA, H_B * (NOPE_B + ROPE_B)), Q_LORA),
        'mla_kv_norm_g': gain((N_ATT_LAYERS, KV_LORA)),
        'mla_w_uk': w((N_ATT_LAYERS, KV_LORA, H_B, NOPE_B), KV_LORA),
        'mla_w_uv': w((N_ATT_LAYERS, KV_LORA, H_B, V_B), KV_LORA),
        'att_w_out': w((N_ATT_LAYERS, ATT_MIX, D_MODEL), ATT_MIX),
        'ret_w_in': w((N_RET_LAYERS, D_MODEL, RET_IN), D_MODEL),
        'ret_gn_g': gain((N_RET_LAYERS, DV_R)),
        'ret_w_out': w((N_RET_LAYERS, RET_MIX, D_MODEL), RET_MIX),
        'final_norm_g': gain((D_MODEL,)),
    }


def reference(x_prompt, x_sample, cache_a_k, cache_a_v, cache_b_lat, cache_b_rope, state_ret, page_table,
              norm_g, ffn_w_gate, ffn_w_up, ffn_w_down, att_w_in, diff_lambda, diff_subln_g,
              mla_q_norm_g, mla_w_uq, mla_kv_norm_g, mla_w_uk, mla_w_uv, att_w_out,
              ret_w_in, ret_gn_g, ret_w_out, final_norm_g):
    dt = x_prompt.dtype
    b_p, s_p = x_prompt.shape[:2]
    t_s = x_sample.shape[1]
    pos_p = jnp.arange(s_p)
    pos_s = PAST_LEN + jnp.arange(t_s)
    xp, xs = x_prompt, x_sample
    ak_p, av_p, lat_p, kr_p, ret_p = [], [], [], [], []
    ak_s, av_s, lat_s, kr_s, ret_s = [], [], [], [], []
    for layer in range(DEPTH):
        j = layer // 2
        xp = _ffn_half(xp, norm_g[layer, 0], ffn_w_gate[layer, 0], ffn_w_up[layer, 0], ffn_w_down[layer, 0])
        xs = _ffn_half(xs, norm_g[layer, 0], ffn_w_gate[layer, 0], ffn_w_up[layer, 0], ffn_w_down[layer, 0])
        hp = _rms_norm(xp, norm_g[layer, 1])
        hs = _rms_norm(xs, norm_g[layer, 1])
        if layer % 2 == 0:
            lam_init = 0.8 - 0.6 * math.exp(-0.3 * layer)
            lam = (jnp.exp(jnp.sum(diff_lambda[j, 0] * diff_lambda[j, 1]).astype(F32))
                   - jnp.exp(jnp.sum(diff_lambda[j, 2] * diff_lambda[j, 3]).astype(F32)) + lam_init)
            qa, ka, va, qn, qr, c, kr = _att_project(hp, pos_p, att_w_in[j], mla_q_norm_g[j], mla_w_uq[j],
                                                      mla_kv_norm_g[j])
            o_a = _diff_attn_prompt(qa, ka, va, lam)
            o_b = _mla_prompt(qn, qr, c, kr, mla_w_uk[j], mla_w_uv[j])
            xp = xp + _att_merge(o_a, o_b, diff_subln_g[j], lam_init, att_w_out[j]).astype(dt)
            ak_p.append(ka)
            av_p.append(va)
            lat_p.append(c)
            kr_p.append(kr)
            qa, ka, va, qn, qr, c, kr = _att_project(hs, pos_s, att_w_in[j], mla_q_norm_g[j], mla_w_uq[j],
                                                      mla_kv_norm_g[j])
            o_a = _diff_attn_sample(qa, ka, va, lam, cache_a_k, cache_a_v, j, page_table)
            o_b = _mla_sample(qn, qr, c, kr, mla_w_uk[j], mla_w_uv[j], cache_b_lat, cache_b_rope, j, page_table)
            xs = xs + _att_merge(o_a, o_b, diff_subln_g[j], lam_init, att_w_out[j]).astype(dt)
            ak_s.append(ka)
            av_s.append(va)
            lat_s.append(c)
            kr_s.append(kr)
        else:
            q, k, v, g = _ret_project(hp, pos_p, ret_w_in[j])
            o, s_fin = _retention(q, k, v, jnp.zeros((b_p, H_R, DK_R, DV_R), F32))
            xp = xp + _ret_merge(o, g, ret_gn_g[j], ret_w_out[j]).astype(dt)
            ret_p.append(s_fin.astype(dt))
            q, k, v, g = _ret_project(hs, pos_s, ret_w_in[j])
            o, s_fin = _retention(q, k, v, state_ret[j])
            xs = xs + _ret_merge(o, g, ret_gn_g[j], ret_w_out[j]).astype(dt)
            ret_s.append(s_fin.astype(dt))
        xp = _ffn_half(xp, norm_g[layer, 2], ffn_w_gate[layer, 1], ffn_w_up[layer, 1], ffn_w_down[layer, 1])
        xs = _ffn_half(xs, norm_g[layer, 2], ffn_w_gate[layer, 1], ffn_w_up[layer, 1], ffn_w_down[layer, 1])
    y_prompt = _rms_norm(xp, final_norm_g)
    y_sample = _rms_norm(xs, final_norm_g)
    return (y_prompt, y_sample,
            jnp.stack(ak_p), jnp.stack(av_p), jnp.stack(lat_p), jnp.stack(kr_p), jnp.stack(ret_p),
            jnp.stack(ak_s), jnp.stack(av_s), jnp.stack(lat_s), jnp.stack(kr_s), jnp.stack(ret_s))
```

```python
import functools
import math

import numpy as np
import jax
import jax.numpy as jnp
from jax import lax
from jax.experimental import pallas as pl
from jax.experimental.pallas import tpu as pltpu

F32 = jnp.float32
BF16 = jnp.bfloat16

RMS_EPS = 1e-6
NEG_INF = -1e30
ROPE_THETA = 500000.0
RET_THETA = 10000.0
RET_CHUNK = 128

LANES = 128
VMEM_LIMIT = 56 * 1024 * 1024


def _cparams(*sem):
    return pltpu.CompilerParams(dimension_semantics=sem, vmem_limit_bytes=VMEM_LIMIT)


def _rms(x, g):
    return x * lax.rsqrt(jnp.mean(x * x, axis=-1, keepdims=True) + RMS_EPS) * g


def _dot(a, b):
    return jnp.dot(a, b, preferred_element_type=F32)


def _dot_nt(a, b):
    return lax.dot_general(a, b, (((1,), (1,)), ((), ())), preferred_element_type=F32)


def _dot_tn(a, b):
    return lax.dot_general(a, b, (((0,), (0,)), ((), ())), preferred_element_type=F32)


def _row_tile(n):
    for t in (512, 256, 128, 64, 32, 16, 8):
        if n % t == 0:
            return t
    raise ValueError(f"row count {n} is not a multiple of 8")


def _resident(shape):
    nd = len(shape)
    return pl.BlockSpec(shape, lambda *_: (0,) * nd, pipeline_mode=pl.Buffered(1))


def _rows(tm, width):
    return pl.BlockSpec((tm, width), lambda i: (i, 0))


def _ffn_body(*refs, nch, final):
    if final:
        x_ref, g_ref, wg_ref, wu_ref, wd_ref, fg_ref, o_ref, h_ref, acc_ref = refs
    else:
        x_ref, g_ref, wg_ref, wu_ref, wd_ref, o_ref, h_ref, acc_ref = refs
    x = x_ref[...]
    h_ref[...] = _rms(x, g_ref[...]).astype(BF16)
    acc_ref[...] = jnp.zeros_like(acc_ref)

    def chunk(c, carry):
        h = h_ref[...]
        a = _dot(h, wg_ref[c])
        u = _dot(h, wu_ref[c])
        act = (a * jax.nn.sigmoid(a) * u).astype(BF16)
        acc_ref[...] += _dot(act, wd_ref[c])
        return carry

    lax.fori_loop(0, nch, chunk, 0)
    y = x + 0.5 * acc_ref[...]
    if final:
        y = _rms(y, fg_ref[...])
    o_ref[...] = y


def _ffn_weights(wg, wu, wd):
    d, dff = wg.shape
    tf = next(t for t in (512, 256, 128) if dff % t == 0)
    nch = dff // tf
    wg_c = wg.astype(BF16).reshape(d, nch, tf).transpose(1, 0, 2)
    wu_c = wu.astype(BF16).reshape(d, nch, tf).transpose(1, 0, 2)
    wd_c = wd.astype(BF16).reshape(nch, tf, d)
    return wg_c, wu_c, wd_c


def _ffn(x, g, w, final_g=None):
    n, d = x.shape
    wg_c, wu_c, wd_c = w
    nch = wg_c.shape[0]
    tm = _row_tile(n)
    final = final_g is not None
    ins = [x, g.reshape(1, d), wg_c, wu_c, wd_c]
    specs = [_rows(tm, d), _resident((1, d)), _resident(wg_c.shape), _resident(wu_c.shape),
             _resident(wd_c.shape)]
    if final:
        ins.append(final_g.reshape(1, d))
        specs.append(_resident((1, d)))
    return pl.pallas_call(
        functools.partial(_ffn_body, nch=nch, final=final),
        grid=(n // tm,),
        in_specs=specs,
        out_specs=_rows(tm, d),
        out_shape=jax.ShapeDtypeStruct((n, d), F32),
        scratch_shapes=[pltpu.VMEM((tm, d), BF16), pltpu.VMEM((tm, d), F32)],
        compiler_params=_cparams("parallel"),
        name="ffn_half",
    )(*ins)


def _rope_lanes(z, c, s_up, s_dn, shift):
    return z * c + pltpu.roll(z, LANES - shift, 1) * s_up + pltpu.roll(z, shift, 1) * s_dn


def _rope_tables_a(pos, hd, rot):
    half = rot // 2
    lane = np.arange(LANES) % hd
    idx = np.where(lane < rot, lane % half, 0)
    inv = 1.0 / (ROPE_THETA ** (jnp.arange(0, rot, 2, dtype=F32) / rot))
    ang = pos.astype(F32)[:, None] * inv[None, :]
    cos = jnp.cos(ang)[:, idx]
    sin = jnp.sin(ang)[:, idx]
    first = jnp.asarray(lane < half)[None, :]
    second = jnp.asarray((lane >= half) & (lane < rot))[None, :]
    c = jnp.where(first | second, cos, 1.0)
    s_up = jnp.where(first, -sin, 0.0)
    s_dn = jnp.where(second, sin, 0.0)
    return c, s_up, s_dn


def _rope_tables_b(pos, off, dim):
    half = dim // 2
    lane = np.arange(LANES)
    rel = lane - off
    idx = np.where((rel >= 0) & (rel < dim), rel % half, 0)
    inv = 1.0 / (ROPE_THETA ** (jnp.arange(0, dim, 2, dtype=F32) / dim))
    ang = pos.astype(F32)[:, None] * inv[None, :]
    cos = jnp.cos(ang)[:, idx]
    sin = jnp.sin(ang)[:, idx]
    first = jnp.asarray((rel >= 0) & (rel < half))[None, :]
    second = jnp.asarray((rel >= half) & (rel < dim))[None, :]
    c = jnp.where(first | second, cos, 1.0)
    s_up = jnp.where(first, -sin, 0.0)
    s_dn = jnp.where(second, sin, 0.0)
    return c, s_up, s_dn


def _att_proj_body(x_ref, g_ref, win_ref, qng_ref, wuq_ref, kvg_ref, wuk_ref, wuv_ref,
                   ca_ref, ua_ref, da_ref, cb_ref, ub_ref, db_ref,
                   qa_o, kaf_o, kab_o, vaf_o, vab_o, qm_o, c_o, kr_o, km_o, vm_o,
                   *, wa, n_hb, rot_a, rope_b, rope_off, scale_a, scale_b):
    h = _rms(x_ref[...], g_ref[...]).astype(BF16)
    ca, ua, da = ca_ref[...], ua_ref[...], da_ref[...]
    cb, ub, db = cb_ref[...], ub_ref[...], db_ref[...]
    sh_a = rot_a // 2
    sh_b = rope_b // 2

    zq = _dot(h, win_ref[:, 0:wa])
    zk = _dot(h, win_ref[:, wa:2 * wa])
    for j in range(wa // LANES):
        sl = slice(j * LANES, (j + 1) * LANES)
        qa_o[:, sl] = (_rope_lanes(zq[:, sl], ca, ua, da, sh_a) * scale_a).astype(BF16)
        kr = _rope_lanes(zk[:, sl], ca, ua, da, sh_a)
        kaf_o[:, sl] = kr
        kab_o[:, sl] = kr.astype(BF16)
    zv = _dot(h, win_ref[:, 2 * wa:3 * wa])
    vaf_o[...] = zv
    vab_o[...] = zv.astype(BF16)

    o = 3 * wa
    ql = qng_ref.shape[1]
    cq = _rms(_dot(h, win_ref[:, o:o + ql]), qng_ref[...]).astype(BF16)
    qb = _dot(cq, wuq_ref[...])
    o += ql
    kvl = kvg_ref.shape[1]
    c = _rms(_dot(h, win_ref[:, o:o + kvl]), kvg_ref[...])
    c_o[...] = c
    cbf = c.astype(BF16)
    o += kvl
    kr = _rope_lanes(_dot(h, win_ref[:, o:o + LANES]), cb, ub, db, sh_b)
    kr_o[...] = kr[:, rope_off:rope_off + rope_b]
    kn = _dot(cbf, wuk_ref[...])
    for j in range(n_hb):
        sl = slice(j * LANES, (j + 1) * LANES)
        qm_o[:, sl] = (_rope_lanes(qb[:, sl], cb, ub, db, sh_b) * scale_b).astype(BF16)
        km_o[:, sl] = (kn[:, sl] + kr).astype(BF16)
    vm_o[...] = _dot(cbf, wuv_ref[...]).astype(BF16)


def _att_weights(w_in, w_uq, w_uk, w_uv, wa, q_lora, kv_lora, rope_b, n_hb, nope_b, v_b):
    d = w_in.shape[0]
    o = 3 * wa + q_lora + kv_lora
    kr_cols = jnp.zeros((d, LANES), F32).at[:, nope_b:nope_b + rope_b].set(w_in[:, o:o + rope_b])
    win_p = jnp.concatenate([w_in[:, :o], kr_cols], axis=1).astype(BF16)
    hq = nope_b + rope_b
    wuq_p = jnp.zeros((q_lora, n_hb, LANES), F32).at[:, :, :hq].set(w_uq.reshape(q_lora, n_hb, hq))
    wuq_p = wuq_p.reshape(q_lora, n_hb * LANES).astype(BF16)
    wuk_p = jnp.zeros((kv_lora, n_hb, LANES), F32).at[:, :, :nope_b].set(w_uk)
    wuk_p = wuk_p.reshape(kv_lora, n_hb * LANES).astype(BF16)
    wuv_p = w_uv.reshape(kv_lora, n_hb * v_b).astype(BF16)
    return win_p, wuq_p, wuk_p, wuv_p


def _att_proj(x, g, weights, qng, kvg, tabs_a, tabs_b, tab_index, *, wa, n_hb, rot_a, rope_b, nope_b, v_b,
              scale_a, scale_b):
    n, d = x.shape
    win_p, wuq_p, wuk_p, wuv_p = weights
    tm = _row_tile(n)
    tab_spec = pl.BlockSpec((tm, LANES), lambda i: (tab_index(i), 0))
    kv_lora = kvg.shape[0]
    out_w = [(wa, BF16), (wa, F32), (wa, BF16), (wa, F32), (wa, BF16), (n_hb * LANES, BF16), (kv_lora, F32),
             (rope_b, F32), (n_hb * LANES, BF16), (n_hb * v_b, BF16)]
    return pl.pallas_call(
        functools.partial(_att_proj_body, wa=wa, n_hb=n_hb, rot_a=rot_a, rope_b=rope_b, rope_off=nope_b,
                          scale_a=scale_a, scale_b=scale_b),
        grid=(n // tm,),
        in_specs=[_rows(tm, d), _resident((1, d)), _resident(win_p.shape), _resident((1, qng.shape[0])),
                  _resident(wuq_p.shape), _resident((1, kv_lora)), _resident(wuk_p.shape),
                  _resident(wuv_p.shape)] + [tab_spec] * 6,
        out_specs=[_rows(tm, w) for w, _ in out_w],
        out_shape=[jax.ShapeDtypeStruct((n, w), dt) for w, dt in out_w],
        compiler_params=_cparams("parallel"),
        name="att_proj",
    )(x, g.reshape(1, d), win_p, qng.reshape(1, -1), wuq_p, kvg.reshape(1, -1), wuk_p, wuv_p,
      *tabs_a, *tabs_b)


def _lam_of(dl, lam_init):
    s01 = jnp.sum(dl[0:1] * dl[1:2], axis=-1, keepdims=True)
    s23 = jnp.sum(dl[2:3] * dl[3:4], axis=-1, keepdims=True)
    return jnp.exp(s01) - jnp.exp(s23) + lam_init


def _flash_body(dl_ref, g_ref, q_ref, k_ref, v_ref, o_ref, *, diff, tq, half, lam_init):
    qi = pl.program_id(2)
    q = q_ref[...]
    lane = lax.broadcasted_iota(jnp.int32, (tq, LANES), 1)
    if diff:
        qs = (jnp.where(lane < half, q, jnp.zeros_like(q)), jnp.where(lane >= half, q, jnp.zeros_like(q)))
        ksl = (slice(0, LANES), slice(0, LANES))
    else:
        qs = (q[:, :LANES], q[:, LANES:])
        ksl = (slice(0, LANES), slice(LANES, 2 * LANES))
    row = lax.broadcasted_iota(jnp.int32, (tq, tq), 0)
    col = lax.broadcasted_iota(jnp.int32, (tq, tq), 1)
    causal = col <= row

    def step(ki, carry, masked):
        ks = pl.multiple_of(ki * tq, tq)
        k = k_ref[pl.ds(ks, tq), :]
        v = v_ref[pl.ds(ks, tq), :]
        out = []
        for a in range(2):
            m, l, acc = carry[a]
            s = _dot_nt(qs[a], k[:, ksl[a]])
            if masked:
                s = jnp.where(causal, s, NEG_INF)
            m_new = jnp.maximum(m, jnp.max(s, axis=-1, keepdims=True))
            corr = jnp.exp(m - m_new)
            p = jnp.exp(s - m_new)
            l = l * corr + jnp.sum(p, axis=-1, keepdims=True)
            acc = acc * corr + _dot(p.astype(BF16), v)
            out.append((m_new, l, acc))
        return tuple(out)

    init = tuple((jnp.full((tq, 1), NEG_INF, F32), jnp.zeros((tq, 1), F32), jnp.zeros((tq, LANES), F32))
                 for _ in range(2))
    carry = lax.fori_loop(0, qi, lambda ki, c: step(ki, c, False), init)
    carry = step(qi, carry, True)
    o0 = carry[0][2] / carry[0][1]
    o1 = carry[1][2] / carry[1][1]
    if diff:
        lam = _lam_of(dl_ref[...], lam_init)
        o = _rms(o0 - lam * o1, g_ref[...]) * (1.0 - lam_init)
    else:
        o = jnp.where(lane < half, o0, o1)
    o_ref[...] = o.astype(o_ref.dtype)


def _flash(q, k, v, dl, g, *, diff, half, lam_init):
    b, s, _ = q.shape
    j = v.shape[2] // LANES
    qw = LANES if diff else 2 * LANES
    tq = next(t for t in (256, 128, 64, 32, 16, 8) if s % t == 0)
    return pl.pallas_call(
        functools.partial(_flash_body, diff=diff, tq=tq, half=half, lam_init=lam_init),
        grid=(b, j, s // tq),
        in_specs=[pl.BlockSpec(dl.shape, lambda bi, ji, qi: (0, 0)),
                  pl.BlockSpec(g.shape, lambda bi, ji, qi: (0, 0)),
                  pl.BlockSpec((None, tq, qw), lambda bi, ji, qi: (bi, qi, ji)),
                  pl.BlockSpec((None, s, qw), lambda bi, ji, qi: (bi, 0, ji)),
                  pl.BlockSpec((None, s, LANES), lambda bi, ji, qi: (bi, 0, ji))],
        out_specs=pl.BlockSpec((None, tq, LANES), lambda bi, ji, qi: (bi, qi, ji)),
        out_shape=jax.ShapeDtypeStruct((b, s, j * LANES), BF16),
        compiler_params=_cparams("parallel", "parallel", "arbitrary"),
        name="flash_diff" if diff else "flash_mla",
    )(dl, g, q, k, v)


def _online(m_sc, l_sc, acc_sc, s, v):
    m_old = m_sc[...]
    m_new = jnp.maximum(m_old, jnp.max(s, axis=-1, keepdims=True))
    corr = jnp.exp(m_old - m_new)
    p = jnp.exp(s - m_new)
    l_sc[...] = l_sc[...] * corr + jnp.sum(p, axis=-1, keepdims=True)
    acc_sc[...] = acc_sc[...] * corr + _dot(p.astype(BF16), v)
    m_sc[...] = m_new


def _diff_dec_body(pt_ref, dl_ref, g_ref, wq_ref, kn_ref, vn_ref, *rest, npg, n_h, t, lam_init):
    k_refs, v_refs = rest[:npg], rest[npg:2 * npg]
    o_ref, m_sc, l_sc, acc_sc = rest[2 * npg:]
    st = pl.program_id(1)
    nrow = wq_ref.shape[0]

    @pl.when(st == 0)
    def _():
        m_sc[...] = jnp.full_like(m_sc, NEG_INF)
        l_sc[...] = jnp.zeros_like(l_sc)
        acc_sc[...] = jnp.zeros_like(acc_sc)

    wq = wq_ref[...]

    def update(k, v, new):
        ncol = k.shape[0]
        row = lax.broadcasted_iota(jnp.int32, (nrow, ncol), 0)
        col = lax.broadcasted_iota(jnp.int32, (nrow, ncol), 1)
        ok = (col % n_h) == ((row // t) % n_h)
        if new:
            ok = ok & ((col // n_h) <= (row % t))
        s = jnp.where(ok, _dot_nt(wq, k), NEG_INF)
        _online(m_sc, l_sc, acc_sc, s, v)

    k = jnp.concatenate([r[...].astype(BF16) for r in k_refs], axis=0)
    v = jnp.concatenate([r[...].astype(BF16) for r in v_refs], axis=0)
    update(k, v, False)

    @pl.when(st == pl.num_programs(1) - 1)
    def _():
        update(kn_ref[...], vn_ref[...], True)
        o = acc_sc[...] / l_sc[...]
        hr = nrow // 2
        lam = _lam_of(dl_ref[...], lam_init)
        o_ref[...] = (_rms(o[:hr] - lam * o[hr:], g_ref[...]) * (1.0 - lam_init)).astype(o_ref.dtype)


def _pages_per_step(n_pages, cap):
    return next(gp for gp in range(min(cap, n_pages), 0, -1) if n_pages % gp == 0)


def _page_spec(rows, width, i, npg, layer):
    return pl.BlockSpec((None, None, rows, width),
                        lambda b, s, pt: (layer, pt[b, s * npg + i], 0, 0))


def _diff_dec(page_table, dl, g, wq, kn, vn, cache_k, cache_v, layer, *, n_h, t, lam_init):
    nb, n_pages = page_table.shape
    npg = _pages_per_step(n_pages, 8)
    nrow = wq.shape[1]
    prow = cache_k.shape[2]
    per_b = lambda shape: pl.BlockSpec((None,) + shape, lambda b, s, pt: (b, 0, 0))
    const = lambda shape: pl.BlockSpec(shape, lambda b, s, pt: (0, 0))
    grid_spec = pltpu.PrefetchScalarGridSpec(
        num_scalar_prefetch=1,
        grid=(nb, n_pages // npg),
        in_specs=[const(dl.shape), const(g.shape), per_b((nrow, LANES)), per_b(kn.shape[1:]), per_b(vn.shape[1:])]
        + [_page_spec(prow, LANES, i, npg, layer) for i in range(npg)] * 2,
        out_specs=per_b((nrow // 2, LANES)),
        scratch_shapes=[pltpu.VMEM((nrow, 1), F32), pltpu.VMEM((nrow, 1), F32), pltpu.VMEM((nrow, LANES), F32)],
    )
    return pl.pallas_call(
        functools.partial(_diff_dec_body, npg=npg, n_h=n_h, t=t, lam_init=lam_init),
        grid_spec=grid_spec,
        out_shape=jax.ShapeDtypeStruct((nb, nrow // 2, LANES), BF16),
        compiler_params=_cparams("parallel", "arbitrary"),
        name="diff_decode",
    )(page_table, dl, g, wq, kn, vn, *([cache_k] * npg), *([cache_v] * npg))


def _mla_dec_body(pt_ref, ql_ref, qr_ref, cn_ref, rn_ref, *rest, npg, t):
    lat_refs, rope_refs = rest[:npg], rest[npg:2 * npg]
    o_ref, m_sc, l_sc, acc_sc = rest[2 * npg:]
    st = pl.program_id(1)
    nrow = ql_ref.shape[0]

    @pl.when(st == 0)
    def _():
        m_sc[...] = jnp.full_like(m_sc, NEG_INF)
        l_sc[...] = jnp.zeros_like(l_sc)
        acc_sc[...] = jnp.zeros_like(acc_sc)

    ql = ql_ref[...]
    qr = qr_ref[...]
    lat = jnp.concatenate([r[...].astype(BF16) for r in lat_refs], axis=0)
    rope_t = jnp.concatenate([r[...].astype(BF16) for r in rope_refs], axis=1)
    _online(m_sc, l_sc, acc_sc, _dot_nt(ql, lat) + _dot(qr, rope_t), lat)

    @pl.when(st == pl.num_programs(1) - 1)
    def _():
        cn = cn_ref[...]
        ncol = cn.shape[0]
        row = lax.broadcasted_iota(jnp.int32, (nrow, ncol), 0)
        col = lax.broadcasted_iota(jnp.int32, (nrow, ncol), 1)
        s = jnp.where(col <= (row % t), _dot_nt(ql, cn) + _dot_nt(qr, rn_ref[...]), NEG_INF)
        _online(m_sc, l_sc, acc_sc, s, cn)
        o_ref[...] = acc_sc[...] / l_sc[...]


def _mla_dec(page_table, ql, qr, cn, rn, cache_lat, cache_rope_t, layer, *, t):
    nb, n_pages = page_table.shape
    npg = _pages_per_step(n_pages, 16)
    nrow, r = ql.shape[1:]
    e = qr.shape[2]
    page = cache_lat.shape[2]
    per_b = lambda shape: pl.BlockSpec((None,) + shape, lambda b, s, pt: (b, 0, 0))
    grid_spec = pltpu.PrefetchScalarGridSpec(
        num_scalar_prefetch=1,
        grid=(nb, n_pages // npg),
        in_specs=[per_b((nrow, r)), per_b((nrow, e)), per_b(cn.shape[1:]), per_b(rn.shape[1:])]
        + [_page_spec(page, r, i, npg, layer) for i in range(npg)]
        + [_page_spec(e, page, i, npg, layer) for i in range(npg)],
        out_specs=per_b((nrow, r)),
        scratch_shapes=[pltpu.VMEM((nrow, 1), F32), pltpu.VMEM((nrow, 1), F32), pltpu.VMEM((nrow, r), F32)],
    )
    return pl.pallas_call(
        functools.partial(_mla_dec_body, npg=npg, t=t),
        grid_spec=grid_spec,
        out_shape=jax.ShapeDtypeStruct((nb, nrow, r), F32),
        compiler_params=_cparams("parallel", "arbitrary"),
        name="mla_decode",
    )(page_table, ql, qr, cn, rn, *([cache_lat] * npg), *([cache_rope_t] * npg))


def _head_mm_body(x_ref, w_ref, o_ref, *, n_h, wi, wo):
    for h in range(n_h):
        o_ref[:, h * wo:(h + 1) * wo] = _dot(x_ref[:, h * wi:(h + 1) * wi], w_ref[h]).astype(o_ref.dtype)


def _head_mm(x, w, out_dtype):
    n = x.shape[0]
    n_h, wi, wo = w.shape
    return pl.pallas_call(
        functools.partial(_head_mm_body, n_h=n_h, wi=wi, wo=wo),
        out_shape=jax.ShapeDtypeStruct((n, n_h * wo), out_dtype),
        compiler_params=pltpu.CompilerParams(vmem_limit_bytes=VMEM_LIMIT),
        name="head_matmul",
    )(x, w)


def _att_merge_body(x_ref, oa_ref, ob_ref, wa_ref, wb_ref, o_ref):
    o_ref[...] = x_ref[...] + _dot(oa_ref[...], wa_ref[...]) + _dot(ob_ref[...], wb_ref[...])


def _att_merge(x, oa, ob, wa, wb):
    n, d = x.shape
    tm = _row_tile(n)
    return pl.pallas_call(
        _att_merge_body,
        grid=(n // tm,),
        in_specs=[_rows(tm, d), _rows(tm, oa.shape[1]), _rows(tm, ob.shape[1]), _resident(wa.shape),
                  _resident(wb.shape)],
        out_specs=_rows(tm, d),
        out_shape=jax.ShapeDtypeStruct((n, d), F32),
        compiler_params=_cparams("parallel"),
        name="att_merge",
    )(x, oa, ob, wa, wb)


def _ret_merge_body(x_ref, y_ref, g_ref, w_ref, o_ref):
    g = g_ref[...]
    z = (y_ref[...].astype(F32) * (g * jax.nn.sigmoid(g))).astype(BF16)
    o_ref[...] = x_ref[...] + _dot(z, w_ref[...])


def _ret_merge(x, y, g, w):
    n, d = x.shape
    tm = _row_tile(n)
    return pl.pallas_call(
        _ret_merge_body,
        grid=(n // tm,),
        in_specs=[_rows(tm, d), _rows(tm, y.shape[1]), _rows(tm, g.shape[1]), _resident(w.shape)],
        out_specs=_rows(tm, d),
        out_shape=jax.ShapeDtypeStruct((n, d), F32),
        compiler_params=_cparams("parallel"),
        name="ret_merge",
    )(x, y, g, w)


def _ret_proj_body(x_ref, g_ref, w_ref, cos_ref, sin_ref, q_o, k_o, v_o, g_o, *, n_h, dk, dv, k_scale):
    h = _rms(x_ref[...], g_ref[...]).astype(BF16)
    cos, sin = cos_ref[...], sin_ref[...]
    half = dk // 2
    wq = n_h * dk
    for part, (out, scale) in enumerate(((q_o, 1.0), (k_o, k_scale))):
        z = _dot(h, w_ref[:, part * wq:(part + 1) * wq])
        for j in range(n_h):
            for c0 in range(0, half, LANES):
                a = slice(j * dk + c0, j * dk + c0 + LANES)
                b = slice(j * dk + half + c0, j * dk + half + c0 + LANES)
                cs, sn = cos[:, c0:c0 + LANES], sin[:, c0:c0 + LANES]
                z1, z2 = z[:, a], z[:, b]
                out[:, a] = ((z1 * cs - z2 * sn) * scale).astype(BF16)
                out[:, b] = ((z1 * sn + z2 * cs) * scale).astype(BF16)
    wv = n_h * dv
    v_o[...] = _dot(h, w_ref[:, 2 * wq:2 * wq + wv]).astype(BF16)
    g_o[...] = _dot(h, w_ref[:, 2 * wq + wv:2 * wq + 2 * wv])


def _ret_proj(x, g, w, cos, sin, tab_index, *, n_h, dk, dv):
    n, d = x.shape
    tm = _row_tile(n)
    half = dk // 2
    tab_spec = pl.BlockSpec((tm, half), lambda i: (tab_index(i), 0))
    out_w = [(n_h * dk, BF16), (n_h * dk, BF16), (n_h * dv, BF16), (n_h * dv, F32)]
    return pl.pallas_call(
        functools.partial(_ret_proj_body, n_h=n_h, dk=dk, dv=dv, k_scale=dk ** -0.5),
        grid=(n // tm,),
        in_specs=[_rows(tm, d), _resident((1, d)), _resident(w.shape), tab_spec, tab_spec],
        out_specs=[_rows(tm, wd) for wd, _ in out_w],
        out_shape=[jax.ShapeDtypeStruct((n, wd), dt) for wd, dt in out_w],
        compiler_params=_cparams("parallel"),
        name="ret_proj",
    )(x, g.reshape(1, d), w, cos, sin)


def _ret_body(*refs, has_s0):
    if has_s0:
        q_ref, k_ref, v_ref, in_ref, qd_ref, kd_ref, cd_ref, gn_ref, s0_ref, o_ref, st_ref = refs
    else:
        q_ref, k_ref, v_ref, in_ref, qd_ref, kd_ref, cd_ref, gn_ref, o_ref, st_ref = refs
    c = pl.program_id(2)

    @pl.when(c == 0)
    def _():
        st_ref[...] = s0_ref[...] if has_s0 else jnp.zeros_like(st_ref)

    q, k, v = q_ref[...], k_ref[...], v_ref[...]
    st = st_ref[...]
    att = (_dot_nt(q, k) * in_ref[...]).astype(BF16)
    o = _dot(att, v) + _dot(q, st.astype(BF16)) * qd_ref[...]
    kd = (k.astype(F32) * kd_ref[...]).astype(BF16)
    st_ref[...] = st * cd_ref[...] + _dot_tn(kd, v)
    o_ref[...] = _rms(o, gn_ref[...]).astype(o_ref.dtype)


def _ret_tables(n_h, c, rows):
    log_g = jnp.log1p(-jnp.exp2(-5.0 - jnp.arange(n_h, dtype=F32)))
    idx = jnp.arange(c, dtype=F32)
    dif = idx[:, None] - idx[None, :]
    intra = jnp.where(dif >= 0, jnp.exp(jnp.maximum(dif, 0.0)[None] * log_g[:, None, None]), 0.0)
    q_dec = jnp.exp((idx + 1.0)[None] * log_g[:, None])[..., None]
    k_dec = jnp.exp((c - 1.0 - idx)[None] * log_g[:, None])[..., None]
    c_dec = jnp.exp(c * log_g)[:, None, None]
    p = rows - c
    return (jnp.pad(intra, ((0, 0), (0, p), (0, p))), jnp.pad(q_dec, ((0, 0), (0, p), (0, 0))),
            jnp.pad(k_dec, ((0, 0), (0, p), (0, 0))), c_dec)


def _retention(q, k, v, tables, gn, s0, *, n_h, c):
    b, t, _ = q.shape
    dk = q.shape[2] // n_h
    dv = v.shape[2] // n_h
    intra, q_dec, k_dec, c_dec = tables
    blk = lambda w: pl.BlockSpec((None, c, w), lambda bi, hi, ci: (bi, ci, hi))
    per_h = lambda shape: pl.BlockSpec((None,) + shape, lambda bi, hi, ci: (hi, 0, 0))
    st_spec = pl.BlockSpec((None, None, dk, dv), lambda bi, hi, ci: (bi, hi, 0, 0))
    ins = [q, k, v, intra, q_dec, k_dec, c_dec, gn.reshape(1, dv)]
    specs = [blk(dk), blk(dk), blk(dv), per_h((c, c)), per_h((c, 1)), per_h((c, 1)), per_h((1, 1)),
             pl.BlockSpec((1, dv), lambda bi, hi, ci: (0, 0))]
    if s0 is not None:
        ins.append(s0)
        specs.append(st_spec)
    return pl.pallas_call(
        functools.partial(_ret_body, has_s0=s0 is not None),
        grid=(b, n_h, t // c),
        in_specs=specs,
        out_specs=[blk(dv), st_spec],
        out_shape=[jax.ShapeDtypeStruct((b, t, n_h * dv), BF16), jax.ShapeDtypeStruct((b, n_h, dk, dv), F32)],
        compiler_params=_cparams("parallel", "parallel", "arbitrary"),
        name="retention",
    )(*ins)


def kernel(x_prompt, x_sample, cache_a_k, cache_a_v, cache_b_lat, cache_b_rope, state_ret, page_table,
           norm_g, ffn_w_gate, ffn_w_up, ffn_w_down, att_w_in, diff_lambda, diff_subln_g,
           mla_q_norm_g, mla_w_uq, mla_kv_norm_g, mla_w_uk, mla_w_uv, att_w_out,
           ret_w_in, ret_gn_g, ret_w_out, final_norm_g):
    bp, sp, d = x_prompt.shape
    bs, ts, _ = x_sample.shape
    depth = norm_g.shape[0]
    n_layers_att, pool, page, h_a, hd2 = cache_a_k.shape
    hd_a = hd2 // 2
    rot_a = hd_a // 4
    wa = h_a * hd2
    kv_lora, h_b, nope_b = mla_w_uk.shape[1:]
    v_b = mla_w_uv.shape[3]
    rope_b = cache_b_rope.shape[3]
    q_lora = mla_q_norm_g.shape[1]
    h_r, dk_r, dv_r = state_ret.shape[2:]
    past = page_table.shape[1] * page
    n_p, n_s = bp * sp, bs * ts

    pos_p = jnp.arange(sp)
    pos_s = past + jnp.arange(ts)
    tm_p, tm_s = _row_tile(n_p), _row_tile(n_s)
    pos_s_rows = jnp.tile(pos_s, tm_s // ts)
    idx_p = lambda i: i % (sp // tm_p)
    idx_s = lambda i: 0

    xp = x_prompt.reshape(n_p, d)
    xs = x_sample.reshape(n_s, d)
    outs_p = {k: [] for k in ("ak", "av", "lat", "kr", "ret")}
    outs_s = {k: [] for k in ("ak", "av", "lat", "kr", "ret")}

    cache_k4 = cache_a_k.reshape(n_layers_att, pool, page * h_a, hd2)
    cache_v4 = cache_a_v.reshape(n_layers_att, pool, page * h_a, hd2)

    for layer in range(depth):
        j = layer // 2
        w1 = _ffn_weights(ffn_w_gate[layer, 0], ffn_w_up[layer, 0], ffn_w_down[layer, 0])
        xp = _ffn(xp, norm_g[layer, 0], w1)
        xs = _ffn(xs, norm_g[layer, 0], w1)
        if layer % 2 == 0:
            lam_init = 0.8 - 0.6 * math.exp(-0.3 * layer)
            scale_a = hd_a ** -0.5
            scale_b = (nope_b + rope_b) ** -0.5
            weights = _att_weights(att_w_in[j], mla_w_uq[j], mla_w_uk[j], mla_w_uv[j], wa, q_lora, kv_lora,
                                   rope_b, h_b, nope_b, v_b)
            proj = functools.partial(_att_proj, weights=weights, qng=mla_q_norm_g[j], kvg=mla_kv_norm_g[j],
                                     wa=wa, n_hb=h_b, rot_a=rot_a, rope_b=rope_b, nope_b=nope_b, v_b=v_b,
                                     scale_a=scale_a, scale_b=scale_b)
            dl = diff_lambda[j]
            sg = diff_subln_g[j].reshape(1, hd2)
            w_oa = att_w_out[j, :wa].astype(BF16)
            w_ob = att_w_out[j, wa:].astype(BF16)

            qa, kaf, kab, vaf, vab, qm, c, kr, km, vm = proj(
                xp, norm_g[layer, 1], tabs_a=_rope_tables_a(pos_p, hd_a, rot_a),
                tabs_b=_rope_tables_b(pos_p, nope_b, rope_b), tab_index=idx_p)
            r3 = lambda z: z.reshape(bp, sp, z.shape[1])
            o_a = _flash(r3(qa), r3(kab), r3(vab), dl, sg, diff=True, half=hd_a, lam_init=lam_init)
            o_b = _flash(r3(qm), r3(km), r3(vm), dl, sg, diff=False, half=v_b, lam_init=lam_init)
            xp = _att_merge(xp, o_a.reshape(n_p, wa), o_b.reshape(n_p, h_b * v_b), w_oa, w_ob)
            outs_p["ak"].append(kaf.reshape(bp, sp, h_a, hd2))
            outs_p["av"].append(vaf.reshape(bp, sp, h_a, hd2))
            outs_p["lat"].append(c.reshape(bp, sp, kv_lora))
            outs_p["kr"].append(kr.reshape(bp, sp, rope_b))

            qa, kaf, kab, vaf, vab, qm, c, kr, km, vm = proj(
                xs, norm_g[layer, 1], tabs_a=_rope_tables_a(pos_s_rows, hd_a, rot_a),
                tabs_b=_rope_tables_b(pos_s_rows, nope_b, rope_b), tab_index=idx_s)
            q5 = qa.reshape(bs, ts, h_a, 2, hd_a).transpose(0, 3, 2, 1, 4)
            eye = jnp.eye(2, dtype=BF16)[None, :, None, None, :, None]
            wq = (q5[:, :, :, :, None, :] * eye).reshape(bs, 2 * h_a * ts, hd2)
            new_rows = -(-ts * h_a // 16) * 16
            pad_new = lambda z: jnp.pad(z.reshape(bs, ts * h_a, hd2), ((0, 0), (0, new_rows - ts * h_a), (0, 0)))
            o_a = _diff_dec(page_table, dl, sg, wq, pad_new(kab), pad_new(vab), cache_k4, cache_v4, j,
                            n_h=h_a, t=ts, lam_init=lam_init)
            o_a = o_a.reshape(bs, h_a, ts, hd2).transpose(0, 2, 1, 3).reshape(n_s, wa)

            w_abs = jnp.zeros((h_b, LANES, kv_lora), F32).at[:, :nope_b].set(mla_w_uk[j].transpose(1, 2, 0))
            q_lat = _head_mm(qm, w_abs.astype(BF16), BF16)
            q_lat = q_lat.reshape(bs, ts, h_b, kv_lora).transpose(0, 2, 1, 3).reshape(bs, h_b * ts, kv_lora)
            q_r = qm.reshape(bs, ts, h_b, LANES)[..., nope_b:nope_b + rope_b]
            q_r = q_r.transpose(0, 2, 1, 3).reshape(bs, h_b * ts, rope_b)
            new_keys = -(-ts // 16) * 16
            pad_keys = lambda z: jnp.pad(z.reshape(bs, ts, -1).astype(BF16), ((0, 0), (0, new_keys - ts), (0, 0)))
            o_lat = _mla_dec(page_table, q_lat, q_r, pad_keys(c), pad_keys(kr), cache_b_lat,
                             jnp.swapaxes(cache_b_rope, 2, 3), j, t=ts)
            o_lat = o_lat.reshape(bs, h_b, ts, kv_lora).transpose(0, 2, 1, 3).reshape(n_s, h_b * kv_lora)
            o_b = _head_mm(o_lat.astype(BF16), mla_w_uv[j].transpose(1, 0, 2).astype(BF16), BF16)
            xs = _att_merge(xs, o_a, o_b, w_oa, w_ob)
            outs_s["ak"].append(kaf.reshape(bs, ts, h_a, hd2))
            outs_s["av"].append(vaf.reshape(bs, ts, h_a, hd2))
            outs_s["lat"].append(c.reshape(bs, ts, kv_lora))
            outs_s["kr"].append(kr.reshape(bs, ts, rope_b))
        else:
            w_in = ret_w_in[j].astype(BF16)
            w_out = ret_w_out[j].astype(BF16)
            inv = 1.0 / (RET_THETA ** (jnp.arange(0, dk_r, 2, dtype=F32) / dk_r))

            def tables(pos):
                ang = pos.astype(F32)[:, None] * inv[None, :]
                return jnp.cos(ang), jnp.sin(ang)

            cos, sin = tables(pos_p)
            q, k, v, g = _ret_proj(xp, norm_g[layer, 1], w_in, cos, sin, idx_p, n_h=h_r, dk=dk_r, dv=dv_r)
            cp = RET_CHUNK if sp % RET_CHUNK == 0 else sp
            r3 = lambda z: z.reshape(bp, sp, z.shape[1])
            o, s_fin = _retention(r3(q), r3(k), r3(v), _ret_tables(h_r, cp, cp), ret_gn_g[j], None, n_h=h_r, c=cp)
            xp = _ret_merge(xp, o.reshape(n_p, h_r * dv_r), g, w_out)
            outs_p["ret"].append(s_fin)

            cos, sin = tables(pos_s_rows)
            q, k, v, g = _ret_proj(xs, norm_g[layer, 1], w_in, cos, sin, idx_s, n_h=h_r, dk=dk_r, dv=dv_r)
            cs = RET_CHUNK if ts % RET_CHUNK == 0 else ts
            rows = -(-cs // 16) * 16
            tpad = (ts // cs) * rows
            padt = lambda z: jnp.pad(z.reshape(bs, ts // cs, cs, z.shape[1]),
                                     ((0, 0), (0, 0), (0, rows - cs), (0, 0))).reshape(bs, tpad, z.shape[1])
            o, s_fin = _retention(padt(q), padt(k), padt(v), _ret_tables(h_r, cs, rows), ret_gn_g[j],
                                  state_ret[j], n_h=h_r, c=rows)
            o = o.reshape(bs, ts // cs, rows, h_r * dv_r)[:, :, :cs].reshape(n_s, h_r * dv_r)
            xs = _ret_merge(xs, o, g, w_out)
            outs_s["ret"].append(s_fin)
        w2 = _ffn_weights(ffn_w_gate[layer, 1], ffn_w_up[layer, 1], ffn_w_down[layer, 1])
        fin = final_norm_g if layer == depth - 1 else None
        xp = _ffn(xp, norm_g[layer, 2], w2, fin)
        xs = _ffn(xs, norm_g[layer, 2], w2, fin)

    return (xp.reshape(bp, sp, d), xs.reshape(bs, ts, d),
            jnp.stack(outs_p["ak"]), jnp.stack(outs_p["av"]), jnp.stack(outs_p["lat"]), jnp.stack(outs_p["kr"]),
            jnp.stack(outs_p["ret"]),
            jnp.stack(outs_s["ak"]), jnp.stack(outs_s["av"]), jnp.stack(outs_s["lat"]), jnp.stack(outs_s["kr"]),
            jnp.stack(outs_s["ret"]))
```

```python
import functools
import math

import numpy as np
import jax
import jax.numpy as jnp
from jax import lax
from jax.experimental import pallas as pl
from jax.experimental.pallas import tpu as pltpu

F32 = jnp.float32
BF16 = jnp.bfloat16

RMS_EPS = 1e-6
NEG_INF = -1e30
ROPE_THETA = 500000.0
RET_THETA = 10000.0
RET_CHUNK = 128
LOG2E = math.log2(math.e)

LANES = 128
VMEM_LIMIT = 56 * 1024 * 1024


def _cparams(*sem):
    return pltpu.CompilerParams(dimension_semantics=sem, vmem_limit_bytes=VMEM_LIMIT)


def _rms(x, g):
    return x * lax.rsqrt(jnp.mean(x * x, axis=-1, keepdims=True) + RMS_EPS) * g


def _dot(a, b):
    return jnp.dot(a, b, preferred_element_type=F32)


def _dot_nt(a, b):
    return lax.dot_general(a, b, (((1,), (1,)), ((), ())), preferred_element_type=F32)


def _dot_tn(a, b):
    return lax.dot_general(a, b, (((0,), (0,)), ((), ())), preferred_element_type=F32)


def _row_tile(n):
    for t in (512, 256, 128, 64, 32, 16, 8):
        if n % t == 0:
            return t
    raise ValueError(f"row count {n} is not a multiple of 8")


def _resident(shape):
    nd = len(shape)
    return pl.BlockSpec(shape, lambda *_: (0,) * nd, pipeline_mode=pl.Buffered(1))


def _rows(tm, width):
    return pl.BlockSpec((tm, width), lambda i: (i, 0))


def _ffn_body(*refs, nch, final):
    if final:
        x_ref, g_ref, wg_ref, wu_ref, wd_ref, fg_ref, o_ref, h_ref, acc_ref = refs
    else:
        x_ref, g_ref, wg_ref, wu_ref, wd_ref, o_ref, h_ref, acc_ref = refs
    x = x_ref[...]
    h_ref[...] = _rms(x, g_ref[...]).astype(BF16)
    acc_ref[...] = jnp.zeros_like(acc_ref)

    def chunk(c, carry):
        h = h_ref[...]
        a = _dot(h, wg_ref[c])
        u = _dot(h, wu_ref[c])
        act = (a * jax.nn.sigmoid(a) * u).astype(BF16)
        acc_ref[...] += _dot(act, wd_ref[c])
        return carry

    lax.fori_loop(0, nch, chunk, 0)
    y = x + 0.5 * acc_ref[...]
    if final:
        y = _rms(y, fg_ref[...])
    o_ref[...] = y


def _ffn_weights(wg, wu, wd):
    d, dff = wg.shape
    tf = next(t for t in (512, 256, 128) if dff % t == 0)
    nch = dff // tf
    wg_c = wg.astype(BF16).reshape(d, nch, tf).transpose(1, 0, 2)
    wu_c = wu.astype(BF16).reshape(d, nch, tf).transpose(1, 0, 2)
    wd_c = wd.astype(BF16).reshape(nch, tf, d)
    return wg_c, wu_c, wd_c


def _ffn(x, g, w, final_g=None):
    n, d = x.shape
    wg_c, wu_c, wd_c = w
    nch = wg_c.shape[0]
    tm = _row_tile(n)
    final = final_g is not None
    ins = [x, g.reshape(1, d), wg_c, wu_c, wd_c]
    specs = [_rows(tm, d), _resident((1, d)), _resident(wg_c.shape), _resident(wu_c.shape),
             _resident(wd_c.shape)]
    if final:
        ins.append(final_g.reshape(1, d))
        specs.append(_resident((1, d)))
    return pl.pallas_call(
        functools.partial(_ffn_body, nch=nch, final=final),
        grid=(n // tm,),
        in_specs=specs,
        out_specs=_rows(tm, d),
        out_shape=jax.ShapeDtypeStruct((n, d), F32),
        scratch_shapes=[pltpu.VMEM((tm, d), BF16), pltpu.VMEM((tm, d), F32)],
        compiler_params=_cparams("parallel"),
        name="ffn_half",
    )(*ins)


def _rope_lanes(z, c, s_up, s_dn, shift):
    return z * c + pltpu.roll(z, LANES - shift, 1) * s_up + pltpu.roll(z, shift, 1) * s_dn


def _rope_tables_a(pos, hd, rot):
    half = rot // 2
    lane = np.arange(LANES) % hd
    idx = np.where(lane < rot, lane % half, 0)
    inv = 1.0 / (ROPE_THETA ** (jnp.arange(0, rot, 2, dtype=F32) / rot))
    ang = pos.astype(F32)[:, None] * inv[None, :]
    cos = jnp.cos(ang)[:, idx]
    sin = jnp.sin(ang)[:, idx]
    first = jnp.asarray(lane < half)[None, :]
    second = jnp.asarray((lane >= half) & (lane < rot))[None, :]
    c = jnp.where(first | second, cos, 1.0)
    s_up = jnp.where(first, -sin, 0.0)
    s_dn = jnp.where(second, sin, 0.0)
    return c, s_up, s_dn


def _rope_tables_b(pos, off, dim):
    half = dim // 2
    lane = np.arange(LANES)
    rel = lane - off
    idx = np.where((rel >= 0) & (rel < dim), rel % half, 0)
    inv = 1.0 / (ROPE_THETA ** (jnp.arange(0, dim, 2, dtype=F32) / dim))
    ang = pos.astype(F32)[:, None] * inv[None, :]
    cos = jnp.cos(ang)[:, idx]
    sin = jnp.sin(ang)[:, idx]
    first = jnp.asarray((rel >= 0) & (rel < half))[None, :]
    second = jnp.asarray((rel >= half) & (rel < dim))[None, :]
    c = jnp.where(first | second, cos, 1.0)
    s_up = jnp.where(first, -sin, 0.0)
    s_dn = jnp.where(second, sin, 0.0)
    return c, s_up, s_dn


def _att_proj_body(x_ref, g_ref, win_ref, qng_ref, wuq_ref, kvg_ref, wuk_ref, wuv_ref,
                   ca_ref, ua_ref, da_ref, cb_ref, ub_ref, db_ref,
                   qa_o, kaf_o, kab_o, vaf_o, vab_o, vat_o, qm_o, c_o, kr_o, km_o, vmt_o,
                   *, wa, n_hb, rot_a, rope_b, rope_off, scale_a, scale_b):
    h = _rms(x_ref[...], g_ref[...]).astype(BF16)
    ca, ua, da = ca_ref[...], ua_ref[...], da_ref[...]
    cb, ub, db = cb_ref[...], ub_ref[...], db_ref[...]
    sh_a = rot_a // 2
    sh_b = rope_b // 2

    zq = _dot(h, win_ref[:, 0:wa])
    zk = _dot(h, win_ref[:, wa:2 * wa])
    for j in range(wa // LANES):
        sl = slice(j * LANES, (j + 1) * LANES)
        qa_o[:, sl] = (_rope_lanes(zq[:, sl], ca, ua, da, sh_a) * scale_a).astype(BF16)
        kr = _rope_lanes(zk[:, sl], ca, ua, da, sh_a)
        kaf_o[:, sl] = kr
        kab_o[:, sl] = kr.astype(BF16)
    zv = _dot(h, win_ref[:, 2 * wa:3 * wa])
    vaf_o[...] = zv
    vab_o[...] = zv.astype(BF16)
    vat_o[...] = zv.T.astype(BF16)

    o = 3 * wa
    ql = qng_ref.shape[1]
    cq = _rms(_dot(h, win_ref[:, o:o + ql]), qng_ref[...]).astype(BF16)
    qb = _dot(cq, wuq_ref[...])
    o += ql
    kvl = kvg_ref.shape[1]
    c = _rms(_dot(h, win_ref[:, o:o + kvl]), kvg_ref[...])
    c_o[...] = c
    cbf = c.astype(BF16)
    o += kvl
    kr = _rope_lanes(_dot(h, win_ref[:, o:o + LANES]), cb, ub, db, sh_b)
    kr_o[...] = kr[:, rope_off:rope_off + rope_b]
    kn = _dot(cbf, wuk_ref[...])
    for j in range(n_hb):
        sl = slice(j * LANES, (j + 1) * LANES)
        qm_o[:, sl] = (_rope_lanes(qb[:, sl], cb, ub, db, sh_b) * scale_b).astype(BF16)
        km_o[:, sl] = (kn[:, sl] + kr).astype(BF16)
    vmt_o[...] = _dot(cbf, wuv_ref[...]).T.astype(BF16)


def _att_weights(w_in, w_uq, w_uk, w_uv, wa, q_lora, kv_lora, rope_b, n_hb, nope_b, v_b):
    d = w_in.shape[0]
    o = 3 * wa + q_lora + kv_lora
    kr_cols = jnp.zeros((d, LANES), F32).at[:, nope_b:nope_b + rope_b].set(w_in[:, o:o + rope_b])
    win_p = jnp.concatenate([w_in[:, :o], kr_cols], axis=1).astype(BF16)
    hq = nope_b + rope_b
    wuq_p = jnp.zeros((q_lora, n_hb, LANES), F32).at[:, :, :hq].set(w_uq.reshape(q_lora, n_hb, hq))
    wuq_p = wuq_p.reshape(q_lora, n_hb * LANES).astype(BF16)
    wuk_p = jnp.zeros((kv_lora, n_hb, LANES), F32).at[:, :, :nope_b].set(w_uk)
    wuk_p = wuk_p.reshape(kv_lora, n_hb * LANES).astype(BF16)
    wuv_p = w_uv.reshape(kv_lora, n_hb * v_b).astype(BF16)
    return win_p, wuq_p, wuk_p, wuv_p


def _att_proj(x, g, weights, qng, kvg, tabs_a, tabs_b, tab_index, *, wa, n_hb, rot_a, rope_b, nope_b, v_b,
              scale_a, scale_b):
    n, d = x.shape
    win_p, wuq_p, wuk_p, wuv_p = weights
    tm = _row_tile(n)
    tab_spec = pl.BlockSpec((tm, LANES), lambda i: (tab_index(i), 0))
    kv_lora = kvg.shape[0]
    row_out = lambda w, dt: (_rows(tm, w), jax.ShapeDtypeStruct((n, w), dt))
    t_out = lambda w: (pl.BlockSpec((None, w, tm), lambda i: (i, 0, 0)), jax.ShapeDtypeStruct((n // tm, w, tm), BF16))
    outs = [row_out(wa, BF16), row_out(wa, F32), row_out(wa, BF16), row_out(wa, F32), row_out(wa, BF16), t_out(wa),
            row_out(n_hb * LANES, BF16), row_out(kv_lora, F32), row_out(rope_b, F32), row_out(n_hb * LANES, BF16),
            t_out(n_hb * v_b)]
    return pl.pallas_call(
        functools.partial(_att_proj_body, wa=wa, n_hb=n_hb, rot_a=rot_a, rope_b=rope_b, rope_off=nope_b,
                          scale_a=scale_a, scale_b=scale_b),
        grid=(n // tm,),
        in_specs=[_rows(tm, d), _resident((1, d)), _resident(win_p.shape), _resident((1, qng.shape[0])),
                  _resident(wuq_p.shape), _resident((1, kv_lora)), _resident(wuk_p.shape),
                  _resident(wuv_p.shape)] + [tab_spec] * 6,
        out_specs=[spec for spec, _ in outs],
        out_shape=[shape for _, shape in outs],
        compiler_params=_cparams("parallel"),
        name="att_proj",
    )(x, g.reshape(1, d), win_p, qng.reshape(1, -1), wuq_p, kvg.reshape(1, -1), wuk_p, wuv_p,
      *tabs_a, *tabs_b)


def _lam_of(dl, lam_init):
    s01 = jnp.sum(dl[0:1] * dl[1:2], axis=-1, keepdims=True)
    s23 = jnp.sum(dl[2:3] * dl[3:4], axis=-1, keepdims=True)
    return jnp.exp(s01) - jnp.exp(s23) + lam_init


def _flash_body(dl_ref, g_ref, q_ref, k_ref, v_ref, o_ref, *, diff, nj, tq, half, lam_init):
    qi = pl.program_id(2)
    qw = LANES if diff else 2 * LANES
    lane = lax.broadcasted_iota(jnp.int32, (tq, LANES), 1)
    qs = []
    for j in range(nj):
        q = q_ref[:, j * qw:(j + 1) * qw]
        if diff:
            qs += [jnp.where(lane < half, q, jnp.zeros_like(q)), jnp.where(lane >= half, q, jnp.zeros_like(q))]
        else:
            qs += [q[:, :LANES], q[:, LANES:]]
    row = lax.broadcasted_iota(jnp.int32, (tq, tq), 0)
    col = lax.broadcasted_iota(jnp.int32, (tq, tq), 1)
    causal = col <= row
    nc = 2 * nj

    def step(ki, carry, masked):
        ks = pl.multiple_of(ki * tq, tq)
        ss = []
        for c in range(nc):
            j, a = divmod(c, 2)
            k0 = j * qw + (0 if diff else a * LANES)
            s = _dot_nt(qs[c], k_ref[pl.ds(ks, tq), k0:k0 + LANES])
            ss.append(jnp.where(causal, s, NEG_INF) if masked else s)
        ps, ms, ls, corrs = [], [], [], []
        for c in range(nc):
            m_old, l_old, _ = carry[c]
            m_new = jnp.maximum(m_old, jnp.max(ss[c], axis=-1, keepdims=True))
            corr = jnp.exp2(m_old - m_new)
            p = jnp.exp2(ss[c] - m_new)
            ls.append(l_old * corr + jnp.sum(p, axis=-1, keepdims=True))
            ps.append(p.astype(BF16))
            ms.append(m_new)
            corrs.append(corr)
        out = []
        for c in range(nc):
            v = v_ref[pl.ds(ks, tq), (c // 2) * LANES:(c // 2 + 1) * LANES]
            out.append((ms[c], ls[c], carry[c][2] * corrs[c] + _dot(ps[c], v)))
        return tuple(out)

    init = tuple((jnp.full((tq, 1), NEG_INF, F32), jnp.zeros((tq, 1), F32), jnp.zeros((tq, LANES), F32))
                 for _ in range(nc))
    carry = lax.fori_loop(0, qi, lambda ki, cr: step(ki, cr, False), init)
    carry = step(qi, carry, True)
    if diff:
        lam = _lam_of(dl_ref[...], lam_init)
    for j in range(nj):
        o0 = carry[2 * j][2] / carry[2 * j][1]
        o1 = carry[2 * j + 1][2] / carry[2 * j + 1][1]
        if diff:
            o = _rms(o0 - lam * o1, g_ref[...]) * (1.0 - lam_init)
        else:
            o = jnp.where(lane < half, o0, o1)
        o_ref[:, j * LANES:(j + 1) * LANES] = o.astype(o_ref.dtype)


FLASH_GROUPS_PER_STEP = 4
FLASH_BLOCK = 256


def _flash(q, k, v, dl, g, *, diff, half, lam_init):
    b, s, _ = q.shape
    jt = v.shape[2] // LANES
    nj = math.gcd(jt, FLASH_GROUPS_PER_STEP)
    qw = LANES if diff else 2 * LANES
    tq = next(t for t in (FLASH_BLOCK, 128, 64, 32, 16, 8) if s % t == 0)
    kv_spec = lambda w: pl.BlockSpec((None, s, nj * w), lambda bi, ji, qi: (bi, 0, ji), pipeline_mode=pl.Buffered(1))
    return pl.pallas_call(
        functools.partial(_flash_body, diff=diff, nj=nj, tq=tq, half=half, lam_init=lam_init),
        grid=(b, jt // nj, s // tq),
        in_specs=[pl.BlockSpec(dl.shape, lambda bi, ji, qi: (0, 0)),
                  pl.BlockSpec(g.shape, lambda bi, ji, qi: (0, 0)),
                  pl.BlockSpec((None, tq, nj * qw), lambda bi, ji, qi: (bi, qi, ji)),
                  kv_spec(qw), kv_spec(LANES)],
        out_specs=pl.BlockSpec((None, tq, nj * LANES), lambda bi, ji, qi: (bi, qi, ji)),
        out_shape=jax.ShapeDtypeStruct((b, s, jt * LANES), BF16),
        compiler_params=_cparams("parallel", "parallel", "arbitrary"),
        name="flash_diff" if diff else "flash_mla",
    )(dl, g, q, k, v)


FLASH_TQ = 256
FLASH_TK = 512


def _flash_t_body(dl_ref, g_ref, q_ref, k_ref, vt_ref, o_ref, *, diff, nj, tq, tk, half, lam_init):
    qi = pl.program_id(2)
    qw = LANES if diff else 2 * LANES
    lane = lax.broadcasted_iota(jnp.int32, (tq, LANES), 1)
    qs = []
    for j in range(nj):
        q = q_ref[:, j * qw:(j + 1) * qw]
        if diff:
            qs += [jnp.where(lane < half, q, jnp.zeros_like(q)), jnp.where(lane >= half, q, jnp.zeros_like(q))]
        else:
            qs += [q[:, :LANES], q[:, LANES:]]
    nc = 2 * nj
    vrows = LANES if diff else half
    q0 = qi * tq
    n_full = q0 // tk

    def step(ki, carry, masked):
        ks = pl.multiple_of(ki * tk, tk)
        ss = []
        for c in range(nc):
            j, a = divmod(c, 2)
            k0 = j * qw + (0 if diff else a * LANES)
            s = _dot_nt(k_ref[pl.ds(ks, tk), k0:k0 + LANES], qs[c])
            if masked:
                key = lax.broadcasted_iota(jnp.int32, (tk, tq), 0) + ks
                qry = lax.broadcasted_iota(jnp.int32, (tk, tq), 1) + q0
                s = jnp.where(key <= qry, s, NEG_INF)
            ss.append(s)
        out = []
        for c in range(nc):
            j, a = divmod(c, 2)
            m_old, l_old, acc = carry[c]
            m_new = jnp.maximum(m_old, jnp.max(ss[c], axis=0, keepdims=True))
            corr = jnp.exp2(m_old - m_new)
            p = jnp.exp2(ss[c] - m_new)
            l_new = l_old * corr + jnp.sum(p, axis=0, keepdims=True)
            r0 = j * LANES + (0 if diff else a * half)
            acc = acc * corr + _dot(vt_ref[ki, r0:r0 + vrows, :], p.astype(BF16))
            out.append((m_new, l_new, acc))
        return tuple(out)

    init = tuple((jnp.full((1, tq), NEG_INF, F32), jnp.zeros((1, tq), F32), jnp.zeros((vrows, tq), F32))
                 for _ in range(nc))
    carry = lax.fori_loop(0, n_full, lambda ki, cr: step(ki, cr, False), init)
    carry = step(n_full, carry, True)
    if diff:
        lam = _lam_of(dl_ref[...], lam_init)
    for j in range(nj):
        o0 = carry[2 * j][2] / carry[2 * j][1]
        o1 = carry[2 * j + 1][2] / carry[2 * j + 1][1]
        if diff:
            d = o0 - lam * o1
            ot = d * lax.rsqrt(jnp.mean(d * d, axis=0, keepdims=True) + RMS_EPS) * g_ref[...] * (1.0 - lam_init)
        else:
            ot = jnp.concatenate([o0, o1], axis=0)
        o_ref[:, j * LANES:(j + 1) * LANES] = ot.T.astype(o_ref.dtype)


def _flash_t(q, k, vt, dl, g_col, *, diff, half, lam_init):
    b, s, _ = q.shape
    tk = vt.shape[2]
    jt = vt.shape[1] // LANES
    nj = math.gcd(jt, 4)
    qw = LANES if diff else 2 * LANES
    tq = min(FLASH_TQ, tk)
    assert s % tk == 0 and tk % tq == 0
    return pl.pallas_call(
        functools.partial(_flash_t_body, diff=diff, nj=nj, tq=tq, tk=tk, half=half, lam_init=lam_init),
        grid=(b, jt // nj, s // tq),
        in_specs=[pl.BlockSpec(dl.shape, lambda bi, ji, qi: (0, 0)),
                  pl.BlockSpec(g_col.shape, lambda bi, ji, qi: (0, 0)),
                  pl.BlockSpec((None, tq, nj * qw), lambda bi, ji, qi: (bi, qi, ji)),
                  pl.BlockSpec((None, s, nj * qw), lambda bi, ji, qi: (bi, 0, ji), pipeline_mode=pl.Buffered(1)),
                  pl.BlockSpec((s // tk, nj * LANES, tk), lambda bi, ji, qi: (bi, ji, 0),
                               pipeline_mode=pl.Buffered(1))],
        out_specs=pl.BlockSpec((None, tq, nj * LANES), lambda bi, ji, qi: (bi, qi, ji)),
        out_shape=jax.ShapeDtypeStruct((b, s, jt * LANES), BF16),
        compiler_params=_cparams("parallel", "parallel", "arbitrary"),
        name="flash_diff" if diff else "flash_mla",
    )(dl, g_col, q, k, vt)


def _online(m_sc, l_sc, acc_sc, s, v):
    m_old = m_sc[...]
    m_new = jnp.maximum(m_old, jnp.max(s, axis=-1, keepdims=True))
    corr = jnp.exp2(m_old - m_new)
    p = jnp.exp2(s - m_new)
    l_sc[...] = l_sc[...] * corr + jnp.sum(p, axis=-1, keepdims=True)
    acc_sc[...] = acc_sc[...] * corr + _dot(p.astype(BF16), v)
    m_sc[...] = m_new


def _page_pipeline(pt_ref, caches, bufs, sems, layer, npg):
    ns = pl.num_programs(1)
    total = pl.num_programs(0) * ns
    g = pl.program_id(0) * ns + pl.program_id(1)
    slot = lax.rem(g, 2)

    def copy(c, page_id, slot_, i):
        return pltpu.make_async_copy(caches[c].at[layer, page_id], bufs[c].at[slot_, i], sems.at[slot_, i, c])

    def start(group, slot_):
        b = group // ns
        first = lax.rem(group, ns) * npg
        for i in range(npg):
            page_id = pt_ref[b, first + i]
            for c in range(len(caches)):
                copy(c, page_id, slot_, i).start()

    @pl.when(g == 0)
    def _():
        start(g, slot)

    @pl.when(g + 1 < total)
    def _():
        start(g + 1, 1 - slot)

    for i in range(npg):
        for c in range(len(caches)):
            copy(c, 0, slot, i).wait()
    return slot


def _pages_per_step(n_pages, cap):
    return next(gp for gp in range(min(cap, n_pages), 0, -1) if n_pages % gp == 0)


def _diff_dec_body(pt_ref, dl_ref, g_ref, wq_ref, kn_ref, vn_ref, ck_hbm, cv_hbm, o_ref,
                   kbuf, vbuf, sems, m_sc, l_sc, acc_sc, *, layer, npg, n_h, t, lam_init):
    slot = _page_pipeline(pt_ref, (ck_hbm, cv_hbm), (kbuf, vbuf), sems, layer, npg)
    st = pl.program_id(1)
    page = kbuf.shape[2] // n_h

    @pl.when(st == 0)
    def _():
        m_sc[...] = jnp.full_like(m_sc, NEG_INF)
        l_sc[...] = jnp.zeros_like(l_sc)
        acc_sc[...] = jnp.zeros_like(acc_sc)

    def head_rows(buf, h):
        return jnp.concatenate([buf[slot, i, pl.ds(h, page, stride=n_h), :].astype(BF16) for i in range(npg)], axis=0)

    for h in range(n_h):
        k = head_rows(kbuf, h)
        _online(m_sc.at[h], l_sc.at[h], acc_sc.at[h], _dot_nt(wq_ref[h], k), head_rows(vbuf, h))

    @pl.when(st == pl.num_programs(1) - 1)
    def _():
        for h in range(n_h):
            kn = kn_ref[h]
            row = lax.broadcasted_iota(jnp.int32, (2 * t, kn.shape[0]), 0)
            col = lax.broadcasted_iota(jnp.int32, (2 * t, kn.shape[0]), 1)
            s = jnp.where(col <= lax.rem(row, t), _dot_nt(wq_ref[h], kn), NEG_INF)
            _online(m_sc.at[h], l_sc.at[h], acc_sc.at[h], s, vn_ref[h])
        o = acc_sc[...] / l_sc[...]
        lam = _lam_of(dl_ref[...], lam_init)
        o_ref[...] = _rms(o[:, :t] - lam * o[:, t:], g_ref[...]) * (1.0 - lam_init)


def _diff_dec(page_table, dl, g, wq, kn, vn, cache_k, cache_v, layer, *, t, lam_init):
    nb, n_pages = page_table.shape
    npg = _pages_per_step(n_pages, 16)
    n_h = wq.shape[1]
    prow = cache_k.shape[2]
    per_b = lambda shape: pl.BlockSpec((None,) + shape, lambda b, s, pt: (b,) + (0,) * len(shape))
    const = lambda shape: pl.BlockSpec(shape, lambda b, s, pt: (0,) * len(shape))
    hbm = pl.BlockSpec(memory_space=pl.ANY)
    grid_spec = pltpu.PrefetchScalarGridSpec(
        num_scalar_prefetch=1,
        grid=(nb, n_pages // npg),
        in_specs=[const(dl.shape), const(g.shape), per_b(wq.shape[1:]), per_b(kn.shape[1:]), per_b(vn.shape[1:]),
                  hbm, hbm],
        out_specs=per_b((n_h, t, LANES)),
        scratch_shapes=[pltpu.VMEM((2, npg, prow, LANES), F32), pltpu.VMEM((2, npg, prow, LANES), F32),
                        pltpu.SemaphoreType.DMA((2, npg, 2)),
                        pltpu.VMEM((n_h, 2 * t, 1), F32), pltpu.VMEM((n_h, 2 * t, 1), F32),
                        pltpu.VMEM((n_h, 2 * t, LANES), F32)],
    )
    return pl.pallas_call(
        functools.partial(_diff_dec_body, layer=layer, npg=npg, n_h=n_h, t=t, lam_init=lam_init),
        grid_spec=grid_spec,
        out_shape=jax.ShapeDtypeStruct((nb, n_h, t, LANES), F32),
        compiler_params=_cparams("arbitrary", "arbitrary"),
        name="diff_decode",
    )(page_table, dl, g, wq, kn, vn, cache_k, cache_v)


def _mla_dec_body(pt_ref, ql_ref, qr_ref, cn_ref, rn_ref, lat_hbm, rope_hbm, o_ref,
                  lbuf, rbuf, sems, m_sc, l_sc, acc_sc, *, layer, npg, t):
    slot = _page_pipeline(pt_ref, (lat_hbm, rope_hbm), (lbuf, rbuf), sems, layer, npg)
    st = pl.program_id(1)
    nrow = ql_ref.shape[0]

    @pl.when(st == 0)
    def _():
        m_sc[...] = jnp.full_like(m_sc, NEG_INF)
        l_sc[...] = jnp.zeros_like(l_sc)
        acc_sc[...] = jnp.zeros_like(acc_sc)

    ql = ql_ref[...]
    qr = qr_ref[...]
    lat = jnp.concatenate([lbuf[slot, i].astype(BF16) for i in range(npg)], axis=0)
    rope_t = jnp.concatenate([rbuf[slot, i].astype(BF16) for i in range(npg)], axis=1)
    _online(m_sc, l_sc, acc_sc, _dot_nt(ql, lat) + _dot(qr, rope_t), lat)

    @pl.when(st == pl.num_programs(1) - 1)
    def _():
        cn = cn_ref[...]
        ncol = cn.shape[0]
        row = lax.broadcasted_iota(jnp.int32, (nrow, ncol), 0)
        col = lax.broadcasted_iota(jnp.int32, (nrow, ncol), 1)
        s = jnp.where(col <= (row % t), _dot_nt(ql, cn) + _dot_nt(qr, rn_ref[...]), NEG_INF)
        _online(m_sc, l_sc, acc_sc, s, cn)
        o_ref[...] = acc_sc[...] / l_sc[...]


def _mla_dec(page_table, ql, qr, cn, rn, cache_lat, cache_rope_t, layer, *, t):
    nb, n_pages = page_table.shape
    npg = _pages_per_step(n_pages, 32)
    nrow, r = ql.shape[1:]
    e = qr.shape[2]
    page = cache_lat.shape[2]
    per_b = lambda shape: pl.BlockSpec((None,) + shape, lambda b, s, pt: (b, 0, 0))
    hbm = pl.BlockSpec(memory_space=pl.ANY)
    grid_spec = pltpu.PrefetchScalarGridSpec(
        num_scalar_prefetch=1,
        grid=(nb, n_pages // npg),
        in_specs=[per_b((nrow, r)), per_b((nrow, e)), per_b(cn.shape[1:]), per_b(rn.shape[1:]), hbm, hbm],
        out_specs=per_b((nrow, r)),
        scratch_shapes=[pltpu.VMEM((2, npg, page, r), F32), pltpu.VMEM((2, npg, e, page), F32),
                        pltpu.SemaphoreType.DMA((2, npg, 2)),
                        pltpu.VMEM((nrow, 1), F32), pltpu.VMEM((nrow, 1), F32), pltpu.VMEM((nrow, r), F32)],
    )
    return pl.pallas_call(
        functools.partial(_mla_dec_body, layer=layer, npg=npg, t=t),
        grid_spec=grid_spec,
        out_shape=jax.ShapeDtypeStruct((nb, nrow, r), F32),
        compiler_params=_cparams("arbitrary", "arbitrary"),
        name="mla_decode",
    )(page_table, ql, qr, cn, rn, cache_lat, cache_rope_t)


def _head_mm_body(x_ref, w_ref, o_ref, *, n_h, wi, wo):
    for h in range(n_h):
        o_ref[:, h * wo:(h + 1) * wo] = _dot(x_ref[:, h * wi:(h + 1) * wi], w_ref[h]).astype(o_ref.dtype)


def _head_mm(x, w, out_dtype):
    n = x.shape[0]
    n_h, wi, wo = w.shape
    return pl.pallas_call(
        functools.partial(_head_mm_body, n_h=n_h, wi=wi, wo=wo),
        out_shape=jax.ShapeDtypeStruct((n, n_h * wo), out_dtype),
        compiler_params=pltpu.CompilerParams(vmem_limit_bytes=VMEM_LIMIT),
        name="head_matmul",
    )(x, w)


def _att_merge_body(x_ref, oa_ref, ob_ref, wa_ref, wb_ref, o_ref):
    o_ref[...] = x_ref[...] + _dot(oa_ref[...], wa_ref[...]) + _dot(ob_ref[...], wb_ref[...])


def _att_merge(x, oa, ob, wa, wb):
    n, d = x.shape
    tm = _row_tile(n)
    return pl.pallas_call(
        _att_merge_body,
        grid=(n // tm,),
        in_specs=[_rows(tm, d), _rows(tm, oa.shape[1]), _rows(tm, ob.shape[1]), _resident(wa.shape),
                  _resident(wb.shape)],
        out_specs=_rows(tm, d),
        out_shape=jax.ShapeDtypeStruct((n, d), F32),
        compiler_params=_cparams("parallel"),
        name="att_merge",
    )(x, oa, ob, wa, wb)


def _ret_merge_body(x_ref, y_ref, g_ref, w_ref, o_ref):
    g = g_ref[...]
    z = (y_ref[...].astype(F32) * (g * jax.nn.sigmoid(g))).astype(BF16)
    o_ref[...] = x_ref[...] + _dot(z, w_ref[...])


def _ret_merge(x, y, g, w):
    n, d = x.shape
    tm = _row_tile(n)
    return pl.pallas_call(
        _ret_merge_body,
        grid=(n // tm,),
        in_specs=[_rows(tm, d), _rows(tm, y.shape[1]), _rows(tm, g.shape[1]), _resident(w.shape)],
        out_specs=_rows(tm, d),
        out_shape=jax.ShapeDtypeStruct((n, d), F32),
        compiler_params=_cparams("parallel"),
        name="ret_merge",
    )(x, y, g, w)


def _ret_proj_body(x_ref, g_ref, w_ref, cos_ref, sin_ref, q_o, k_o, v_o, g_o, *, n_h, dk, dv, k_scale):
    h = _rms(x_ref[...], g_ref[...]).astype(BF16)
    cos, sin = cos_ref[...], sin_ref[...]
    half = dk // 2
    wq = n_h * dk
    for part, (out, scale) in enumerate(((q_o, 1.0), (k_o, k_scale))):
        z = _dot(h, w_ref[:, part * wq:(part + 1) * wq])
        for j in range(n_h):
            for c0 in range(0, half, LANES):
                a = slice(j * dk + c0, j * dk + c0 + LANES)
                b = slice(j * dk + half + c0, j * dk + half + c0 + LANES)
                cs, sn = cos[:, c0:c0 + LANES], sin[:, c0:c0 + LANES]
                z1, z2 = z[:, a], z[:, b]
                out[:, a] = ((z1 * cs - z2 * sn) * scale).astype(BF16)
                out[:, b] = ((z1 * sn + z2 * cs) * scale).astype(BF16)
    wv = n_h * dv
    v_o[...] = _dot(h, w_ref[:, 2 * wq:2 * wq + wv]).astype(BF16)
    g_o[...] = _dot(h, w_ref[:, 2 * wq + wv:2 * wq + 2 * wv])


def _ret_proj(x, g, w, cos, sin, tab_index, *, n_h, dk, dv):
    n, d = x.shape
    tm = _row_tile(n)
    half = dk // 2
    tab_spec = pl.BlockSpec((tm, half), lambda i: (tab_index(i), 0))
    out_w = [(n_h * dk, BF16), (n_h * dk, BF16), (n_h * dv, BF16), (n_h * dv, F32)]
    return pl.pallas_call(
        functools.partial(_ret_proj_body, n_h=n_h, dk=dk, dv=dv, k_scale=dk ** -0.5),
        grid=(n // tm,),
        in_specs=[_rows(tm, d), _resident((1, d)), _resident(w.shape), tab_spec, tab_spec],
        out_specs=[_rows(tm, wd) for wd, _ in out_w],
        out_shape=[jax.ShapeDtypeStruct((n, wd), dt) for wd, dt in out_w],
        compiler_params=_cparams("parallel"),
        name="ret_proj",
    )(x, g.reshape(1, d), w, cos, sin)


def _ret_body(*refs, has_s0):
    if has_s0:
        q_ref, k_ref, v_ref, in_ref, qd_ref, kd_ref, cd_ref, gn_ref, s0_ref, o_ref, st_ref = refs
    else:
        q_ref, k_ref, v_ref, in_ref, qd_ref, kd_ref, cd_ref, gn_ref, o_ref, st_ref = refs
    c = pl.program_id(2)

    @pl.when(c == 0)
    def _():
        st_ref[...] = s0_ref[...] if has_s0 else jnp.zeros_like(st_ref)

    q, k, v = q_ref[...], k_ref[...], v_ref[...]
    st = st_ref[...]
    att = (_dot_nt(q, k) * in_ref[...]).astype(BF16)
    o = _dot(att, v) + _dot(q, st.astype(BF16)) * qd_ref[...]
    kd = (k.astype(F32) * kd_ref[...]).astype(BF16)
    st_ref[...] = st * cd_ref[...] + _dot_tn(kd, v)
    o_ref[...] = _rms(o, gn_ref[...]).astype(o_ref.dtype)


def _ret_tables(n_h, c, rows):
    log_g = jnp.log1p(-jnp.exp2(-5.0 - jnp.arange(n_h, dtype=F32)))
    idx = jnp.arange(c, dtype=F32)
    dif = idx[:, None] - idx[None, :]
    intra = jnp.where(dif >= 0, jnp.exp(jnp.maximum(dif, 0.0)[None] * log_g[:, None, None]), 0.0)
    q_dec = jnp.exp((idx + 1.0)[None] * log_g[:, None])[..., None]
    k_dec = jnp.exp((c - 1.0 - idx)[None] * log_g[:, None])[..., None]
    c_dec = jnp.exp(c * log_g)[:, None, None]
    p = rows - c
    return (jnp.pad(intra, ((0, 0), (0, p), (0, p))), jnp.pad(q_dec, ((0, 0), (0, p), (0, 0))),
            jnp.pad(k_dec, ((0, 0), (0, p), (0, 0))), c_dec)


def _retention(q, k, v, tables, gn, s0, *, n_h, c):
    b, t, _ = q.shape
    dk = q.shape[2] // n_h
    dv = v.shape[2] // n_h
    intra, q_dec, k_dec, c_dec = tables
    blk = lambda w: pl.BlockSpec((None, c, w), lambda bi, hi, ci: (bi, ci, hi))
    per_h = lambda shape: pl.BlockSpec((None,) + shape, lambda bi, hi, ci: (hi, 0, 0))
    st_spec = pl.BlockSpec((None, None, dk, dv), lambda bi, hi, ci: (bi, hi, 0, 0))
    ins = [q, k, v, intra, q_dec, k_dec, c_dec, gn.reshape(1, dv)]
    specs = [blk(dk), blk(dk), blk(dv), per_h((c, c)), per_h((c, 1)), per_h((c, 1)), per_h((1, 1)),
             pl.BlockSpec((1, dv), lambda bi, hi, ci: (0, 0))]
    if s0 is not None:
        ins.append(s0)
        specs.append(st_spec)
    return pl.pallas_call(
        functools.partial(_ret_body, has_s0=s0 is not None),
        grid=(b, n_h, t // c),
        in_specs=specs,
        out_specs=[blk(dv), st_spec],
        out_shape=[jax.ShapeDtypeStruct((b, t, n_h * dv), BF16), jax.ShapeDtypeStruct((b, n_h, dk, dv), F32)],
        compiler_params=_cparams("parallel", "parallel", "arbitrary"),
        name="retention",
    )(*ins)


def kernel(x_prompt, x_sample, cache_a_k, cache_a_v, cache_b_lat, cache_b_rope, state_ret, page_table,
           norm_g, ffn_w_gate, ffn_w_up, ffn_w_down, att_w_in, diff_lambda, diff_subln_g,
           mla_q_norm_g, mla_w_uq, mla_kv_norm_g, mla_w_uk, mla_w_uv, att_w_out,
           ret_w_in, ret_gn_g, ret_w_out, final_norm_g):
    bp, sp, d = x_prompt.shape
    bs, ts, _ = x_sample.shape
    depth = norm_g.shape[0]
    n_layers_att, pool, page, h_a, hd2 = cache_a_k.shape
    hd_a = hd2 // 2
    rot_a = hd_a // 4
    wa = h_a * hd2
    kv_lora, h_b, nope_b = mla_w_uk.shape[1:]
    v_b = mla_w_uv.shape[3]
    rope_b = cache_b_rope.shape[3]
    q_lora = mla_q_norm_g.shape[1]
    h_r, dk_r, dv_r = state_ret.shape[2:]
    past = page_table.shape[1] * page
    n_p, n_s = bp * sp, bs * ts

    pos_p = jnp.arange(sp)
    pos_s = past + jnp.arange(ts)
    tm_p, tm_s = _row_tile(n_p), _row_tile(n_s)
    pos_s_rows = jnp.tile(pos_s, tm_s // ts)
    idx_p = lambda i: i % (sp // tm_p)
    idx_s = lambda i: 0

    xp = x_prompt.reshape(n_p, d)
    xs = x_sample.reshape(n_s, d)
    outs_p = {k: [] for k in ("ak", "av", "lat", "kr", "ret")}
    outs_s = {k: [] for k in ("ak", "av", "lat", "kr", "ret")}

    cache_k4 = cache_a_k.reshape(n_layers_att, pool, page * h_a, hd2)
    cache_v4 = cache_a_v.reshape(n_layers_att, pool, page * h_a, hd2)

    for layer in range(depth):
        j = layer // 2
        w1 = _ffn_weights(ffn_w_gate[layer, 0], ffn_w_up[layer, 0], ffn_w_down[layer, 0])
        xp = _ffn(xp, norm_g[layer, 0], w1)
        xs = _ffn(xs, norm_g[layer, 0], w1)
        if layer % 2 == 0:
            lam_init = 0.8 - 0.6 * math.exp(-0.3 * layer)
            scale_a = hd_a ** -0.5 * LOG2E
            scale_b = (nope_b + rope_b) ** -0.5 * LOG2E
            weights = _att_weights(att_w_in[j], mla_w_uq[j], mla_w_uk[j], mla_w_uv[j], wa, q_lora, kv_lora,
                                   rope_b, h_b, nope_b, v_b)
            proj = functools.partial(_att_proj, weights=weights, qng=mla_q_norm_g[j], kvg=mla_kv_norm_g[j],
                                     wa=wa, n_hb=h_b, rot_a=rot_a, rope_b=rope_b, nope_b=nope_b, v_b=v_b,
                                     scale_a=scale_a, scale_b=scale_b)
            dl = diff_lambda[j]
            sg = diff_subln_g[j].reshape(1, hd2)
            w_oa = att_w_out[j, :wa].astype(BF16)
            w_ob = att_w_out[j, wa:].astype(BF16)

            qa, kaf, kab, vaf, _, vat, qm, c, kr, km, vmt = proj(
                xp, norm_g[layer, 1], tabs_a=_rope_tables_a(pos_p, hd_a, rot_a),
                tabs_b=_rope_tables_b(pos_p, nope_b, rope_b), tab_index=idx_p)
            r3 = lambda z: z.reshape(bp, sp, z.shape[1])
            sg_col = diff_subln_g[j].reshape(hd2, 1)
            o_a = _flash_t(r3(qa), r3(kab), vat, dl, sg_col, diff=True, half=hd_a, lam_init=lam_init)
            o_b = _flash_t(r3(qm), r3(km), vmt, dl, sg_col, diff=False, half=v_b, lam_init=lam_init)
            xp = _att_merge(xp, o_a.reshape(n_p, wa), o_b.reshape(n_p, h_b * v_b), w_oa, w_ob)
            outs_p["ak"].append(kaf.reshape(bp, sp, h_a, hd2))
            outs_p["av"].append(vaf.reshape(bp, sp, h_a, hd2))
            outs_p["lat"].append(c.reshape(bp, sp, kv_lora))
            outs_p["kr"].append(kr.reshape(bp, sp, rope_b))

            qa, kaf, kab, vaf, vab, _, qm, c, kr, _, _ = proj(
                xs, norm_g[layer, 1], tabs_a=_rope_tables_a(pos_s_rows, hd_a, rot_a),
                tabs_b=_rope_tables_b(pos_s_rows, nope_b, rope_b), tab_index=idx_s)
            q5 = qa.reshape(bs, ts, h_a, 2, hd_a).transpose(0, 2, 3, 1, 4)
            eye = jnp.eye(2, dtype=BF16)[None, None, :, None, :, None]
            wq = (q5[:, :, :, :, None, :] * eye).reshape(bs, h_a, 2 * ts, hd2)
            new_keys = -(-ts // 16) * 16
            pad_new = lambda z: jnp.pad(z.reshape(bs, ts, h_a, hd2).transpose(0, 2, 1, 3),
                                        ((0, 0), (0, 0), (0, new_keys - ts), (0, 0)))
            o_a = _diff_dec(page_table, dl, sg, wq, pad_new(kab), pad_new(vab), cache_k4, cache_v4, j,
                            t=ts, lam_init=lam_init)
            o_a = o_a.transpose(0, 2, 1, 3).reshape(n_s, wa).astype(BF16)

            w_abs = jnp.zeros((h_b, LANES, kv_lora), F32).at[:, :nope_b].set(mla_w_uk[j].transpose(1, 2, 0))
            q_lat = _head_mm(qm, w_abs.astype(BF16), BF16)
            q_lat = q_lat.reshape(bs, ts, h_b, kv_lora).transpose(0, 2, 1, 3).reshape(bs, h_b * ts, kv_lora)
            q_r = qm.reshape(bs, ts, h_b, LANES)[..., nope_b:nope_b + rope_b]
            q_r = q_r.transpose(0, 2, 1, 3).reshape(bs, h_b * ts, rope_b)
            new_keys = -(-ts // 16) * 16
            pad_keys = lambda z: jnp.pad(z.reshape(bs, ts, -1).astype(BF16), ((0, 0), (0, new_keys - ts), (0, 0)))
            o_lat = _mla_dec(page_table, q_lat, q_r, pad_keys(c), pad_keys(kr), cache_b_lat,
                             jnp.swapaxes(cache_b_rope, 2, 3), j, t=ts)
            o_lat = o_lat.reshape(bs, h_b, ts, kv_lora).transpose(0, 2, 1, 3).reshape(n_s, h_b * kv_lora)
            o_b = _head_mm(o_lat.astype(BF16), mla_w_uv[j].transpose(1, 0, 2).astype(BF16), BF16)
            xs = _att_merge(xs, o_a, o_b, w_oa, w_ob)
            outs_s["ak"].append(kaf.reshape(bs, ts, h_a, hd2))
            outs_s["av"].append(vaf.reshape(bs, ts, h_a, hd2))
            outs_s["lat"].append(c.reshape(bs, ts, kv_lora))
            outs_s["kr"].append(kr.reshape(bs, ts, rope_b))
        else:
            w_in = ret_w_in[j].astype(BF16)
            w_out = ret_w_out[j].astype(BF16)
            inv = 1.0 / (RET_THETA ** (jnp.arange(0, dk_r, 2, dtype=F32) / dk_r))

            def tables(pos):
                ang = pos.astype(F32)[:, None] * inv[None, :]
                return jnp.cos(ang), jnp.sin(ang)

            cos, sin = tables(pos_p)
            q, k, v, g = _ret_proj(xp, norm_g[layer, 1], w_in, cos, sin, idx_p, n_h=h_r, dk=dk_r, dv=dv_r)
            cp = RET_CHUNK if sp % RET_CHUNK == 0 else sp
            r3 = lambda z: z.reshape(bp, sp, z.shape[1])
            o, s_fin = _retention(r3(q), r3(k), r3(v), _ret_tables(h_r, cp, cp), ret_gn_g[j], None, n_h=h_r, c=cp)
            xp = _ret_merge(xp, o.reshape(n_p, h_r * dv_r), g, w_out)
            outs_p["ret"].append(s_fin)

            cos, sin = tables(pos_s_rows)
            q, k, v, g = _ret_proj(xs, norm_g[layer, 1], w_in, cos, sin, idx_s, n_h=h_r, dk=dk_r, dv=dv_r)
            cs = RET_CHUNK if ts % RET_CHUNK == 0 else ts
            rows = -(-cs // 16) * 16
            tpad = (ts // cs) * rows
            padt = lambda z: jnp.pad(z.reshape(bs, ts // cs, cs, z.shape[1]),
                                     ((0, 0), (0, 0), (0, rows - cs), (0, 0))).reshape(bs, tpad, z.shape[1])
            o, s_fin = _retention(padt(q), padt(k), padt(v), _ret_tables(h_r, cs, rows), ret_gn_g[j],
                                  state_ret[j], n_h=h_r, c=rows)
            o = o.reshape(bs, ts // cs, rows, h_r * dv_r)[:, :, :cs].reshape(n_s, h_r * dv_r)
            xs = _ret_merge(xs, o, g, w_out)
            outs_s["ret"].append(s_fin)
        w2 = _ffn_weights(ffn_w_gate[layer, 1], ffn_w_up[layer, 1], ffn_w_down[layer, 1])
        fin = final_norm_g if layer == depth - 1 else None
        xp = _ffn(xp, norm_g[layer, 2], w2, fin)
        xs = _ffn(xs, norm_g[layer, 2], w2, fin)

    return (xp.reshape(bp, sp, d), xs.reshape(bs, ts, d),
            jnp.stack(outs_p["ak"]), jnp.stack(outs_p["av"]), jnp.stack(outs_p["lat"]), jnp.stack(outs_p["kr"]),
            jnp.stack(outs_p["ret"]),
            jnp.stack(outs_s["ak"]), jnp.stack(outs_s["av"]), jnp.stack(outs_s["lat"]), jnp.stack(outs_s["kr"]),
            jnp.stack(outs_s["ret"]))
```

```python
import functools
import math

import numpy as np
import jax
import jax.numpy as jnp
from jax import lax
from jax.experimental import pallas as pl
from jax.experimental.pallas import tpu as pltpu

F32 = jnp.float32
BF16 = jnp.bfloat16

RMS_EPS = 1e-6
NEG_INF = -1e30
ROPE_THETA = 500000.0
RET_THETA = 10000.0
RET_CHUNK = 128
LOG2E = math.log2(math.e)

LANES = 128
VMEM_LIMIT = 56 * 1024 * 1024


def _cparams(*sem):
    return pltpu.CompilerParams(dimension_semantics=sem, vmem_limit_bytes=VMEM_LIMIT)


def _rms(x, g):
    return x * lax.rsqrt(jnp.mean(x * x, axis=-1, keepdims=True) + RMS_EPS) * g


def _dot(a, b):
    return jnp.dot(a, b, preferred_element_type=F32)


def _dot_nt(a, b):
    return lax.dot_general(a, b, (((1,), (1,)), ((), ())), preferred_element_type=F32)


def _dot_tn(a, b):
    return lax.dot_general(a, b, (((0,), (0,)), ((), ())), preferred_element_type=F32)


def _row_tile(n):
    for t in (512, 256, 128, 64, 32, 16, 8):
        if n % t == 0:
            return t
    raise ValueError(f"row count {n} is not a multiple of 8")


def _resident(shape):
    nd = len(shape)
    return pl.BlockSpec(shape, lambda *_: (0,) * nd, pipeline_mode=pl.Buffered(1))


def _rows(tm, width):
    return pl.BlockSpec((tm, width), lambda i: (i, 0))


def _ffn_body(*refs, tf, final):
    if final:
        x_ref, g_ref, wg_ref, wu_ref, wd_ref, fg_ref, o_ref, h_ref, act_ref = refs
    else:
        x_ref, g_ref, wg_ref, wu_ref, wd_ref, o_ref, h_ref, act_ref = refs
    x = x_ref[...]
    h_ref[...] = _rms(x, g_ref[...]).astype(BF16)
    for c0 in range(0, act_ref.shape[1], tf):
        h = h_ref[...]
        a = _dot(h, wg_ref[:, c0:c0 + tf])
        u = _dot(h, wu_ref[:, c0:c0 + tf])
        act_ref[:, c0:c0 + tf] = (a * jax.nn.sigmoid(a) * u).astype(BF16)
    y = x + 0.5 * _dot(act_ref[...], wd_ref[...])
    if final:
        y = _rms(y, fg_ref[...])
    o_ref[...] = y


def _ffn_weights(wg, wu, wd):
    return wg.astype(BF16), wu.astype(BF16), wd.astype(BF16)


def _ffn(x, g, w, final_g=None):
    n, d = x.shape
    wg, wu, wd = w
    dff = wg.shape[1]
    tf = next(t for t in (512, 256, 128) if dff % t == 0)
    tm = _row_tile(n)
    final = final_g is not None
    ins = [x, g.reshape(1, d), wg, wu, wd]
    specs = [_rows(tm, d), _resident((1, d)), _resident(wg.shape), _resident(wu.shape), _resident(wd.shape)]
    if final:
        ins.append(final_g.reshape(1, d))
        specs.append(_resident((1, d)))
    return pl.pallas_call(
        functools.partial(_ffn_body, tf=tf, final=final),
        grid=(n // tm,),
        in_specs=specs,
        out_specs=_rows(tm, d),
        out_shape=jax.ShapeDtypeStruct((n, d), F32),
        scratch_shapes=[pltpu.VMEM((tm, d), BF16), pltpu.VMEM((tm, dff), BF16)],
        compiler_params=_cparams("parallel"),
        name="ffn_half",
    )(*ins)


def _rope_lanes(z, c, s_up, s_dn, shift):
    return z * c + pltpu.roll(z, LANES - shift, 1) * s_up + pltpu.roll(z, shift, 1) * s_dn


def _rope_tables_a(pos, hd, rot):
    half = rot // 2
    lane = np.arange(LANES) % hd
    idx = np.where(lane < rot, lane % half, 0)
    inv = 1.0 / (ROPE_THETA ** (jnp.arange(0, rot, 2, dtype=F32) / rot))
    ang = pos.astype(F32)[:, None] * inv[None, :]
    cos = jnp.cos(ang)[:, idx]
    sin = jnp.sin(ang)[:, idx]
    first = jnp.asarray(lane < half)[None, :]
    second = jnp.asarray((lane >= half) & (lane < rot))[None, :]
    c = jnp.where(first | second, cos, 1.0)
    s_up = jnp.where(first, -sin, 0.0)
    s_dn = jnp.where(second, sin, 0.0)
    return c, s_up, s_dn


def _rope_tables_b(pos, off, dim):
    half = dim // 2
    lane = np.arange(LANES)
    rel = lane - off
    idx = np.where((rel >= 0) & (rel < dim), rel % half, 0)
    inv = 1.0 / (ROPE_THETA ** (jnp.arange(0, dim, 2, dtype=F32) / dim))
    ang = pos.astype(F32)[:, None] * inv[None, :]
    cos = jnp.cos(ang)[:, idx]
    sin = jnp.sin(ang)[:, idx]
    first = jnp.asarray((rel >= 0) & (rel < half))[None, :]
    second = jnp.asarray((rel >= half) & (rel < dim))[None, :]
    c = jnp.where(first | second, cos, 1.0)
    s_up = jnp.where(first, -sin, 0.0)
    s_dn = jnp.where(second, sin, 0.0)
    return c, s_up, s_dn


def _att_proj_body(x_ref, g_ref, win_ref, qng_ref, wuq_ref, kvg_ref, wuk_ref, wuv_ref,
                   ca_ref, ua_ref, da_ref, cb_ref, ub_ref, db_ref,
                   qa_o, kaf_o, kab_o, vaf_o, vab_o, vat_o, qm_o, c_o, kr_o, km_o, vmt_o,
                   *, wa, n_hb, rot_a, rope_b, rope_off, scale_a, scale_b):
    h = _rms(x_ref[...], g_ref[...]).astype(BF16)
    ca, ua, da = ca_ref[...], ua_ref[...], da_ref[...]
    cb, ub, db = cb_ref[...], ub_ref[...], db_ref[...]
    sh_a = rot_a // 2
    sh_b = rope_b // 2

    zq = _dot(h, win_ref[:, 0:wa])
    zk = _dot(h, win_ref[:, wa:2 * wa])
    for j in range(wa // LANES):
        sl = slice(j * LANES, (j + 1) * LANES)
        qa_o[:, sl] = (_rope_lanes(zq[:, sl], ca, ua, da, sh_a) * scale_a).astype(BF16)
        kr = _rope_lanes(zk[:, sl], ca, ua, da, sh_a)
        kaf_o[:, sl] = kr
        kab_o[:, sl] = kr.astype(BF16)
    zv = _dot(h, win_ref[:, 2 * wa:3 * wa])
    vaf_o[...] = zv
    vab_o[...] = zv.astype(BF16)
    vat_o[...] = zv.T.astype(BF16)

    o = 3 * wa
    ql = qng_ref.shape[1]
    cq = _rms(_dot(h, win_ref[:, o:o + ql]), qng_ref[...]).astype(BF16)
    qb = _dot(cq, wuq_ref[...])
    o += ql
    kvl = kvg_ref.shape[1]
    c = _rms(_dot(h, win_ref[:, o:o + kvl]), kvg_ref[...])
    c_o[...] = c
    cbf = c.astype(BF16)
    o += kvl
    kr = _rope_lanes(_dot(h, win_ref[:, o:o + LANES]), cb, ub, db, sh_b)
    kr_o[...] = kr[:, rope_off:rope_off + rope_b]
    kn = _dot(cbf, wuk_ref[...])
    for j in range(n_hb):
        sl = slice(j * LANES, (j + 1) * LANES)
        qm_o[:, sl] = (_rope_lanes(qb[:, sl], cb, ub, db, sh_b) * scale_b).astype(BF16)
        km_o[:, sl] = (kn[:, sl] + kr).astype(BF16)
    vmt_o[...] = _dot(cbf, wuv_ref[...]).T.astype(BF16)


def _att_weights(w_in, w_uq, w_uk, w_uv, wa, q_lora, kv_lora, rope_b, n_hb, nope_b, v_b):
    d = w_in.shape[0]
    o = 3 * wa + q_lora + kv_lora
    kr_cols = jnp.zeros((d, LANES), F32).at[:, nope_b:nope_b + rope_b].set(w_in[:, o:o + rope_b])
    win_p = jnp.concatenate([w_in[:, :o], kr_cols], axis=1).astype(BF16)
    hq = nope_b + rope_b
    wuq_p = jnp.zeros((q_lora, n_hb, LANES), F32).at[:, :, :hq].set(w_uq.reshape(q_lora, n_hb, hq))
    wuq_p = wuq_p.reshape(q_lora, n_hb * LANES).astype(BF16)
    wuk_p = jnp.zeros((kv_lora, n_hb, LANES), F32).at[:, :, :nope_b].set(w_uk)
    wuk_p = wuk_p.reshape(kv_lora, n_hb * LANES).astype(BF16)
    wuv_p = w_uv.reshape(kv_lora, n_hb * v_b).astype(BF16)
    return win_p, wuq_p, wuk_p, wuv_p


def _att_proj(x, g, weights, qng, kvg, tabs_a, tabs_b, tab_index, *, wa, n_hb, rot_a, rope_b, nope_b, v_b,
              scale_a, scale_b):
    n, d = x.shape
    win_p, wuq_p, wuk_p, wuv_p = weights
    tm = _row_tile(n)
    tab_spec = pl.BlockSpec((tm, LANES), lambda i: (tab_index(i), 0))
    kv_lora = kvg.shape[0]
    row_out = lambda w, dt: (_rows(tm, w), jax.ShapeDtypeStruct((n, w), dt))
    t_out = lambda w: (pl.BlockSpec((None, w, tm), lambda i: (i, 0, 0)), jax.ShapeDtypeStruct((n // tm, w, tm), BF16))
    outs = [row_out(wa, BF16), row_out(wa, F32), row_out(wa, BF16), row_out(wa, F32), row_out(wa, BF16), t_out(wa),
            row_out(n_hb * LANES, BF16), row_out(kv_lora, F32), row_out(rope_b, F32), row_out(n_hb * LANES, BF16),
            t_out(n_hb * v_b)]
    return pl.pallas_call(
        functools.partial(_att_proj_body, wa=wa, n_hb=n_hb, rot_a=rot_a, rope_b=rope_b, rope_off=nope_b,
                          scale_a=scale_a, scale_b=scale_b),
        grid=(n // tm,),
        in_specs=[_rows(tm, d), _resident((1, d)), _resident(win_p.shape), _resident((1, qng.shape[0])),
                  _resident(wuq_p.shape), _resident((1, kv_lora)), _resident(wuk_p.shape),
                  _resident(wuv_p.shape)] + [tab_spec] * 6,
        out_specs=[spec for spec, _ in outs],
        out_shape=[shape for _, shape in outs],
        compiler_params=_cparams("parallel"),
        name="att_proj",
    )(x, g.reshape(1, d), win_p, qng.reshape(1, -1), wuq_p, kvg.reshape(1, -1), wuk_p, wuv_p,
      *tabs_a, *tabs_b)


def _lam_of(dl, lam_init):
    s01 = jnp.sum(dl[0:1] * dl[1:2], axis=-1, keepdims=True)
    s23 = jnp.sum(dl[2:3] * dl[3:4], axis=-1, keepdims=True)
    return jnp.exp(s01) - jnp.exp(s23) + lam_init


FLASH_TQ = 512
FLASH_TK = 512


def _flash_t_body(dl_ref, g_ref, q_ref, k_ref, vt_ref, o_ref, *, diff, nj, tq, tk, half, lam_init):
    qi = pl.program_id(2)
    qw = LANES if diff else 2 * LANES
    lane = lax.broadcasted_iota(jnp.int32, (tq, LANES), 1)
    qs = []
    for j in range(nj):
        q = q_ref[:, j * qw:(j + 1) * qw]
        if diff:
            qs += [jnp.where(lane < half, q, jnp.zeros_like(q)), jnp.where(lane >= half, q, jnp.zeros_like(q))]
        else:
            qs += [q[:, :LANES], q[:, LANES:]]
    nc = 2 * nj
    vrows = LANES if diff else half
    q0 = qi * tq
    n_full = q0 // tk

    def step(ki, carry, masked):
        ks = pl.multiple_of(ki * tk, tk)
        ss = []
        for c in range(nc):
            j, a = divmod(c, 2)
            k0 = j * qw + (0 if diff else a * LANES)
            s = _dot_nt(k_ref[pl.ds(ks, tk), k0:k0 + LANES], qs[c])
            if masked:
                key = lax.broadcasted_iota(jnp.int32, (tk, tq), 0) + ks
                qry = lax.broadcasted_iota(jnp.int32, (tk, tq), 1) + q0
                s = jnp.where(key <= qry, s, NEG_INF)
            ss.append(s)
        out = []
        for c in range(nc):
            j, a = divmod(c, 2)
            m_old, l_old, acc = carry[c]
            m_new = jnp.maximum(m_old, jnp.max(ss[c], axis=0, keepdims=True))
            corr = jnp.exp2(m_old - m_new)
            p = jnp.exp2(ss[c] - m_new)
            l_new = l_old * corr + jnp.sum(p, axis=0, keepdims=True)
            r0 = j * LANES + (0 if diff else a * half)
            acc = acc * corr + _dot(vt_ref[ki, r0:r0 + vrows, :], p.astype(BF16))
            out.append((m_new, l_new, acc))
        return tuple(out)

    init = tuple((jnp.full((1, tq), NEG_INF, F32), jnp.zeros((1, tq), F32), jnp.zeros((vrows, tq), F32))
                 for _ in range(nc))
    carry = lax.fori_loop(0, n_full, lambda ki, cr: step(ki, cr, False), init)
    carry = step(n_full, carry, True)
    if diff:
        lam = _lam_of(dl_ref[...], lam_init)
    for j in range(nj):
        o0 = carry[2 * j][2] / carry[2 * j][1]
        o1 = carry[2 * j + 1][2] / carry[2 * j + 1][1]
        if diff:
            d = o0 - lam * o1
            ot = d * lax.rsqrt(jnp.mean(d * d, axis=0, keepdims=True) + RMS_EPS) * g_ref[...] * (1.0 - lam_init)
        else:
            ot = jnp.concatenate([o0, o1], axis=0)
        o_ref[:, j * LANES:(j + 1) * LANES] = ot.T.astype(o_ref.dtype)


def _flash_t(q, k, vt, dl, g_col, *, diff, half, lam_init):
    b, s, _ = q.shape
    tk = vt.shape[2]
    jt = vt.shape[1] // LANES
    nj = math.gcd(jt, 4)
    qw = LANES if diff else 2 * LANES
    tq = min(FLASH_TQ, tk)
    assert s % tk == 0 and tk % tq == 0
    return pl.pallas_call(
        functools.partial(_flash_t_body, diff=diff, nj=nj, tq=tq, tk=tk, half=half, lam_init=lam_init),
        grid=(b, jt // nj, s // tq),
        in_specs=[pl.BlockSpec(dl.shape, lambda bi, ji, qi: (0, 0)),
                  pl.BlockSpec(g_col.shape, lambda bi, ji, qi: (0, 0)),
                  pl.BlockSpec((None, tq, nj * qw), lambda bi, ji, qi: (bi, qi, ji)),
                  pl.BlockSpec((None, s, nj * qw), lambda bi, ji, qi: (bi, 0, ji), pipeline_mode=pl.Buffered(1)),
                  pl.BlockSpec((s // tk, nj * LANES, tk), lambda bi, ji, qi: (bi, ji, 0),
                               pipeline_mode=pl.Buffered(1))],
        out_specs=pl.BlockSpec((None, tq, nj * LANES), lambda bi, ji, qi: (bi, qi, ji)),
        out_shape=jax.ShapeDtypeStruct((b, s, jt * LANES), BF16),
        compiler_params=_cparams("parallel", "parallel", "arbitrary"),
        name="flash_diff" if diff else "flash_mla",
    )(dl, g_col, q, k, vt)


def _online(m_sc, l_sc, acc_sc, s, v):
    m_old = m_sc[...]
    m_new = jnp.maximum(m_old, jnp.max(s, axis=-1, keepdims=True))
    corr = jnp.exp2(m_old - m_new)
    p = jnp.exp2(s - m_new)
    l_sc[...] = l_sc[...] * corr + jnp.sum(p, axis=-1, keepdims=True)
    acc_sc[...] = acc_sc[...] * corr + _dot(p.astype(BF16), v)
    m_sc[...] = m_new


def _page_pipeline(pt_ref, caches, bufs, sems, layer, npg):
    ns = pl.num_programs(1)
    total = pl.num_programs(0) * ns
    g = pl.program_id(0) * ns + pl.program_id(1)
    slot = lax.rem(g, 2)

    def copy(c, page_id, slot_, i):
        return pltpu.make_async_copy(caches[c].at[layer, page_id], bufs[c].at[slot_, i], sems.at[slot_, i, c])

    def start(group, slot_):
        b = group // ns
        first = lax.rem(group, ns) * npg
        for i in range(npg):
            page_id = pt_ref[b, first + i]
            for c in range(len(caches)):
                copy(c, page_id, slot_, i).start()

    @pl.when(g == 0)
    def _():
        start(g, slot)

    @pl.when(g + 1 < total)
    def _():
        start(g + 1, 1 - slot)

    for i in range(npg):
        for c in range(len(caches)):
            copy(c, 0, slot, i).wait()
    return slot


def _pages_per_step(n_pages, cap):
    return next(gp for gp in range(min(cap, n_pages), 0, -1) if n_pages % gp == 0)


def _diff_dec_body(pt_ref, dl_ref, g_ref, wq_ref, kn_ref, vn_ref, ck_hbm, cv_hbm, o_ref,
                   kbuf, vbuf, sems, m_sc, l_sc, acc_sc, *, layer, npg, n_h, t, lam_init):
    slot = _page_pipeline(pt_ref, (ck_hbm, cv_hbm), (kbuf, vbuf), sems, layer, npg)
    st = pl.program_id(1)
    page = kbuf.shape[2] // n_h

    @pl.when(st == 0)
    def _():
        m_sc[...] = jnp.full_like(m_sc, NEG_INF)
        l_sc[...] = jnp.zeros_like(l_sc)
        acc_sc[...] = jnp.zeros_like(acc_sc)

    def head_rows(buf, h):
        return jnp.concatenate([buf[slot, i, pl.ds(h, page, stride=n_h), :].astype(BF16) for i in range(npg)], axis=0)

    m_old, l_old, acc_old = m_sc[...], l_sc[...], acc_sc[...]
    ss = [_dot_nt(wq_ref[h], head_rows(kbuf, h)) for h in range(n_h)]
    m_new = [jnp.maximum(m_old[h], jnp.max(ss[h], axis=-1, keepdims=True)) for h in range(n_h)]
    ps = [jnp.exp2(ss[h] - m_new[h]) for h in range(n_h)]
    pv = [_dot(ps[h].astype(BF16), head_rows(vbuf, h)) for h in range(n_h)]
    for h in range(n_h):
        corr = jnp.exp2(m_old[h] - m_new[h])
        m_sc[h] = m_new[h]
        l_sc[h] = l_old[h] * corr + jnp.sum(ps[h], axis=-1, keepdims=True)
        acc_sc[h] = acc_old[h] * corr + pv[h]

    @pl.when(st == pl.num_programs(1) - 1)
    def _():
        for h in range(n_h):
            kn = kn_ref[h]
            row = lax.broadcasted_iota(jnp.int32, (2 * t, kn.shape[0]), 0)
            col = lax.broadcasted_iota(jnp.int32, (2 * t, kn.shape[0]), 1)
            s = jnp.where(col <= lax.rem(row, t), _dot_nt(wq_ref[h], kn), NEG_INF)
            _online(m_sc.at[h], l_sc.at[h], acc_sc.at[h], s, vn_ref[h])
        o = acc_sc[...] / l_sc[...]
        lam = _lam_of(dl_ref[...], lam_init)
        o_ref[...] = _rms(o[:, :t] - lam * o[:, t:], g_ref[...]) * (1.0 - lam_init)


def _diff_dec(page_table, dl, g, wq, kn, vn, cache_k, cache_v, layer, *, t, lam_init):
    nb, n_pages = page_table.shape
    npg = _pages_per_step(n_pages, 16)
    n_h = wq.shape[1]
    prow = cache_k.shape[2]
    per_b = lambda shape: pl.BlockSpec((None,) + shape, lambda b, s, pt: (b,) + (0,) * len(shape))
    const = lambda shape: pl.BlockSpec(shape, lambda b, s, pt: (0,) * len(shape))
    hbm = pl.BlockSpec(memory_space=pl.ANY)
    grid_spec = pltpu.PrefetchScalarGridSpec(
        num_scalar_prefetch=1,
        grid=(nb, n_pages // npg),
        in_specs=[const(dl.shape), const(g.shape), per_b(wq.shape[1:]), per_b(kn.shape[1:]), per_b(vn.shape[1:]),
                  hbm, hbm],
        out_specs=per_b((n_h, t, LANES)),
        scratch_shapes=[pltpu.VMEM((2, npg, prow, LANES), F32), pltpu.VMEM((2, npg, prow, LANES), F32),
                        pltpu.SemaphoreType.DMA((2, npg, 2)),
                        pltpu.VMEM((n_h, 2 * t, 1), F32), pltpu.VMEM((n_h, 2 * t, 1), F32),
                        pltpu.VMEM((n_h, 2 * t, LANES), F32)],
    )
    return pl.pallas_call(
        functools.partial(_diff_dec_body, layer=layer, npg=npg, n_h=n_h, t=t, lam_init=lam_init),
        grid_spec=grid_spec,
        out_shape=jax.ShapeDtypeStruct((nb, n_h, t, LANES), F32),
        compiler_params=_cparams("arbitrary", "arbitrary"),
        name="diff_decode",
    )(page_table, dl, g, wq, kn, vn, cache_k, cache_v)


def _mla_dec_body(pt_ref, ql_ref, qr_ref, cn_ref, rn_ref, lat_hbm, rope_hbm, o_ref,
                  lbuf, rbuf, sems, m_sc, l_sc, acc_sc, *, layer, npg, t):
    slot = _page_pipeline(pt_ref, (lat_hbm, rope_hbm), (lbuf, rbuf), sems, layer, npg)
    st = pl.program_id(1)
    nrow = ql_ref.shape[0]

    @pl.when(st == 0)
    def _():
        m_sc[...] = jnp.full_like(m_sc, NEG_INF)
        l_sc[...] = jnp.zeros_like(l_sc)
        acc_sc[...] = jnp.zeros_like(acc_sc)

    ql = ql_ref[...]
    qr = qr_ref[...]
    lat = jnp.concatenate([lbuf[slot, i].astype(BF16) for i in range(npg)], axis=0)
    rope_t = jnp.concatenate([rbuf[slot, i].astype(BF16) for i in range(npg)], axis=1)
    _online(m_sc, l_sc, acc_sc, _dot_nt(ql, lat) + _dot(qr, rope_t), lat)

    @pl.when(st == pl.num_programs(1) - 1)
    def _():
        cn = cn_ref[...]
        ncol = cn.shape[0]
        row = lax.broadcasted_iota(jnp.int32, (nrow, ncol), 0)
        col = lax.broadcasted_iota(jnp.int32, (nrow, ncol), 1)
        s = jnp.where(col <= (row % t), _dot_nt(ql, cn) + _dot_nt(qr, rn_ref[...]), NEG_INF)
        _online(m_sc, l_sc, acc_sc, s, cn)
        o_ref[...] = acc_sc[...] / l_sc[...]


def _mla_dec(page_table, ql, qr, cn, rn, cache_lat, cache_rope_t, layer, *, t):
    nb, n_pages = page_table.shape
    npg = _pages_per_step(n_pages, 32)
    nrow, r = ql.shape[1:]
    e = qr.shape[2]
    page = cache_lat.shape[2]
    per_b = lambda shape: pl.BlockSpec((None,) + shape, lambda b, s, pt: (b, 0, 0))
    hbm = pl.BlockSpec(memory_space=pl.ANY)
    grid_spec = pltpu.PrefetchScalarGridSpec(
        num_scalar_prefetch=1,
        grid=(nb, n_pages // npg),
        in_specs=[per_b((nrow, r)), per_b((nrow, e)), per_b(cn.shape[1:]), per_b(rn.shape[1:]), hbm, hbm],
        out_specs=per_b((nrow, r)),
        scratch_shapes=[pltpu.VMEM((2, npg, page, r), F32), pltpu.VMEM((2, npg, e, page), F32),
                        pltpu.SemaphoreType.DMA((2, npg, 2)),
                        pltpu.VMEM((nrow, 1), F32), pltpu.VMEM((nrow, 1), F32), pltpu.VMEM((nrow, r), F32)],
    )
    return pl.pallas_call(
        functools.partial(_mla_dec_body, layer=layer, npg=npg, t=t),
        grid_spec=grid_spec,
        out_shape=jax.ShapeDtypeStruct((nb, nrow, r), F32),
        compiler_params=_cparams("arbitrary", "arbitrary"),
        name="mla_decode",
    )(page_table, ql, qr, cn, rn, cache_lat, cache_rope_t)


def _head_mm_body(x_ref, w_ref, o_ref, *, n_h, wi, wo):
    for h in range(n_h):
        o_ref[:, h * wo:(h + 1) * wo] = _dot(x_ref[:, h * wi:(h + 1) * wi], w_ref[h]).astype(o_ref.dtype)


def _head_mm(x, w, out_dtype):
    n = x.shape[0]
    n_h, wi, wo = w.shape
    return pl.pallas_call(
        functools.partial(_head_mm_body, n_h=n_h, wi=wi, wo=wo),
        out_shape=jax.ShapeDtypeStruct((n, n_h * wo), out_dtype),
        compiler_params=pltpu.CompilerParams(vmem_limit_bytes=VMEM_LIMIT),
        name="head_matmul",
    )(x, w)


def _att_merge_body(x_ref, oa_ref, ob_ref, wa_ref, wb_ref, o_ref):
    o_ref[...] = x_ref[...] + _dot(oa_ref[...], wa_ref[...]) + _dot(ob_ref[...], wb_ref[...])


def _att_merge(x, oa, ob, wa, wb):
    n, d = x.shape
    tm = _row_tile(n)
    return pl.pallas_call(
        _att_merge_body,
        grid=(n // tm,),
        in_specs=[_rows(tm, d), _rows(tm, oa.shape[1]), _rows(tm, ob.shape[1]), _resident(wa.shape),
                  _resident(wb.shape)],
        out_specs=_rows(tm, d),
        out_shape=jax.ShapeDtypeStruct((n, d), F32),
        compiler_params=_cparams("parallel"),
        name="att_merge",
    )(x, oa, ob, wa, wb)


def _ret_merge_body(x_ref, y_ref, g_ref, w_ref, o_ref):
    g = g_ref[...]
    z = (y_ref[...].astype(F32) * (g * jax.nn.sigmoid(g))).astype(BF16)
    o_ref[...] = x_ref[...] + _dot(z, w_ref[...])


def _ret_merge(x, y, g, w):
    n, d = x.shape
    tm = _row_tile(n)
    return pl.pallas_call(
        _ret_merge_body,
        grid=(n // tm,),
        in_specs=[_rows(tm, d), _rows(tm, y.shape[1]), _rows(tm, g.shape[1]), _resident(w.shape)],
        out_specs=_rows(tm, d),
        out_shape=jax.ShapeDtypeStruct((n, d), F32),
        compiler_params=_cparams("parallel"),
        name="ret_merge",
    )(x, y, g, w)


def _ret_proj_body(x_ref, g_ref, w_ref, cos_ref, sin_ref, q_o, k_o, v_o, g_o, *, n_h, dk, dv, k_scale):
    h = _rms(x_ref[...], g_ref[...]).astype(BF16)
    cos, sin = cos_ref[...], sin_ref[...]
    half = dk // 2
    wq = n_h * dk
    for part, (out, scale) in enumerate(((q_o, 1.0), (k_o, k_scale))):
        z = _dot(h, w_ref[:, part * wq:(part + 1) * wq])
        for j in range(n_h):
            for c0 in range(0, half, LANES):
                a = slice(j * dk + c0, j * dk + c0 + LANES)
                b = slice(j * dk + half + c0, j * dk + half + c0 + LANES)
                cs, sn = cos[:, c0:c0 + LANES], sin[:, c0:c0 + LANES]
                z1, z2 = z[:, a], z[:, b]
                out[:, a] = ((z1 * cs - z2 * sn) * scale).astype(BF16)
                out[:, b] = ((z1 * sn + z2 * cs) * scale).astype(BF16)
    wv = n_h * dv
    v_o[...] = _dot(h, w_ref[:, 2 * wq:2 * wq + wv]).astype(BF16)
    g_o[...] = _dot(h, w_ref[:, 2 * wq + wv:2 * wq + 2 * wv])


def _ret_proj(x, g, w, cos, sin, tab_index, *, n_h, dk, dv):
    n, d = x.shape
    tm = _row_tile(n)
    half = dk // 2
    tab_spec = pl.BlockSpec((tm, half), lambda i: (tab_index(i), 0))
    out_w = [(n_h * dk, BF16), (n_h * dk, BF16), (n_h * dv, BF16), (n_h * dv, F32)]
    return pl.pallas_call(
        functools.partial(_ret_proj_body, n_h=n_h, dk=dk, dv=dv, k_scale=dk ** -0.5),
        grid=(n // tm,),
        in_specs=[_rows(tm, d), _resident((1, d)), _resident(w.shape), tab_spec, tab_spec],
        out_specs=[_rows(tm, wd) for wd, _ in out_w],
        out_shape=[jax.ShapeDtypeStruct((n, wd), dt) for wd, dt in out_w],
        compiler_params=_cparams("parallel"),
        name="ret_proj",
    )(x, g.reshape(1, d), w, cos, sin)


def _ret_body(*refs, has_s0):
    if has_s0:
        q_ref, k_ref, v_ref, in_ref, qd_ref, kd_ref, cd_ref, gn_ref, s0_ref, o_ref, st_ref = refs
    else:
        q_ref, k_ref, v_ref, in_ref, qd_ref, kd_ref, cd_ref, gn_ref, o_ref, st_ref = refs
    c = pl.program_id(1)
    n_h, dk, dv = st_ref.shape

    @pl.when(c == 0)
    def _():
        st_ref[...] = s0_ref[...] if has_s0 else jnp.zeros_like(st_ref)

    sts = [st_ref[h] for h in range(n_h)]
    new = []
    for h in range(n_h):
        q, k = q_ref[:, h * dk:(h + 1) * dk], k_ref[:, h * dk:(h + 1) * dk]
        v = v_ref[:, h * dv:(h + 1) * dv]
        att = (_dot_nt(q, k) * in_ref[h]).astype(BF16)
        o = _dot(att, v) + _dot(q, sts[h].astype(BF16)) * qd_ref[h]
        kd = (k.astype(F32) * kd_ref[h]).astype(BF16)
        new.append(sts[h] * cd_ref[h] + _dot_tn(kd, v))
        o_ref[:, h * dv:(h + 1) * dv] = _rms(o, gn_ref[...]).astype(o_ref.dtype)
    for h in range(n_h):
        st_ref[h] = new[h]


def _ret_tables(n_h, c, rows):
    log_g = jnp.log1p(-jnp.exp2(-5.0 - jnp.arange(n_h, dtype=F32)))
    idx = jnp.arange(c, dtype=F32)
    dif = idx[:, None] - idx[None, :]
    intra = jnp.where(dif >= 0, jnp.exp(jnp.maximum(dif, 0.0)[None] * log_g[:, None, None]), 0.0)
    q_dec = jnp.exp((idx + 1.0)[None] * log_g[:, None])[..., None]
    k_dec = jnp.exp((c - 1.0 - idx)[None] * log_g[:, None])[..., None]
    c_dec = jnp.exp(c * log_g)[:, None, None]
    p = rows - c
    return (jnp.pad(intra, ((0, 0), (0, p), (0, p))), jnp.pad(q_dec, ((0, 0), (0, p), (0, 0))),
            jnp.pad(k_dec, ((0, 0), (0, p), (0, 0))), c_dec)


def _retention(q, k, v, tables, gn, s0, *, n_h, c):
    b, t, _ = q.shape
    dk = q.shape[2] // n_h
    dv = v.shape[2] // n_h
    intra, q_dec, k_dec, c_dec = tables
    blk = lambda w: pl.BlockSpec((None, c, n_h * w), lambda bi, ci: (bi, ci, 0))
    table = lambda shape: pl.BlockSpec(shape, lambda bi, ci: (0,) * len(shape))
    st_spec = pl.BlockSpec((None, n_h, dk, dv), lambda bi, ci: (bi, 0, 0, 0))
    ins = [q, k, v, intra, q_dec, k_dec, c_dec, gn.reshape(1, dv)]
    specs = [blk(dk), blk(dk), blk(dv), table(intra.shape), table(q_dec.shape), table(k_dec.shape),
             table(c_dec.shape), table((1, dv))]
    if s0 is not None:
        ins.append(s0)
        specs.append(st_spec)
    return pl.pallas_call(
        functools.partial(_ret_body, has_s0=s0 is not None),
        grid=(b, t // c),
        in_specs=specs,
        out_specs=[blk(dv), st_spec],
        out_shape=[jax.ShapeDtypeStruct((b, t, n_h * dv), BF16), jax.ShapeDtypeStruct((b, n_h, dk, dv), F32)],
        compiler_params=_cparams("parallel", "arbitrary"),
        name="retention",
    )(*ins)


def kernel(x_prompt, x_sample, cache_a_k, cache_a_v, cache_b_lat, cache_b_rope, state_ret, page_table,
           norm_g, ffn_w_gate, ffn_w_up, ffn_w_down, att_w_in, diff_lambda, diff_subln_g,
           mla_q_norm_g, mla_w_uq, mla_kv_norm_g, mla_w_uk, mla_w_uv, att_w_out,
           ret_w_in, ret_gn_g, ret_w_out, final_norm_g):
    bp, sp, d = x_prompt.shape
    bs, ts, _ = x_sample.shape
    depth = norm_g.shape[0]
    n_layers_att, pool, page, h_a, hd2 = cache_a_k.shape
    hd_a = hd2 // 2
    rot_a = hd_a // 4
    wa = h_a * hd2
    kv_lora, h_b, nope_b = mla_w_uk.shape[1:]
    v_b = mla_w_uv.shape[3]
    rope_b = cache_b_rope.shape[3]
    q_lora = mla_q_norm_g.shape[1]
    h_r, dk_r, dv_r = state_ret.shape[2:]
    past = page_table.shape[1] * page
    n_p, n_s = bp * sp, bs * ts

    pos_p = jnp.arange(sp)
    pos_s = past + jnp.arange(ts)
    tm_p, tm_s = _row_tile(n_p), _row_tile(n_s)
    pos_s_rows = jnp.tile(pos_s, tm_s // ts)
    idx_p = lambda i: i % (sp // tm_p)
    idx_s = lambda i: 0

    xp = x_prompt.reshape(n_p, d)
    xs = x_sample.reshape(n_s, d)
    outs_p = {k: [] for k in ("ak", "av", "lat", "kr", "ret")}
    outs_s = {k: [] for k in ("ak", "av", "lat", "kr", "ret")}

    cache_k4 = cache_a_k.reshape(n_layers_att, pool, page * h_a, hd2)
    cache_v4 = cache_a_v.reshape(n_layers_att, pool, page * h_a, hd2)

    for layer in range(depth):
        j = layer // 2
        w1 = _ffn_weights(ffn_w_gate[layer, 0], ffn_w_up[layer, 0], ffn_w_down[layer, 0])
        xp = _ffn(xp, norm_g[layer, 0], w1)
        xs = _ffn(xs, norm_g[layer, 0], w1)
        if layer % 2 == 0:
            lam_init = 0.8 - 0.6 * math.exp(-0.3 * layer)
            scale_a = hd_a ** -0.5 * LOG2E
            scale_b = (nope_b + rope_b) ** -0.5 * LOG2E
            weights = _att_weights(att_w_in[j], mla_w_uq[j], mla_w_uk[j], mla_w_uv[j], wa, q_lora, kv_lora,
                                   rope_b, h_b, nope_b, v_b)
            proj = functools.partial(_att_proj, weights=weights, qng=mla_q_norm_g[j], kvg=mla_kv_norm_g[j],
                                     wa=wa, n_hb=h_b, rot_a=rot_a, rope_b=rope_b, nope_b=nope_b, v_b=v_b,
                                     scale_a=scale_a, scale_b=scale_b)
            dl = diff_lambda[j]
            sg = diff_subln_g[j].reshape(1, hd2)
            w_oa = att_w_out[j, :wa].astype(BF16)
            w_ob = att_w_out[j, wa:].astype(BF16)

            qa, kaf, kab, vaf, _, vat, qm, c, kr, km, vmt = proj(
                xp, norm_g[layer, 1], tabs_a=_rope_tables_a(pos_p, hd_a, rot_a),
                tabs_b=_rope_tables_b(pos_p, nope_b, rope_b), tab_index=idx_p)
            r3 = lambda z: z.reshape(bp, sp, z.shape[1])
            sg_col = diff_subln_g[j].reshape(hd2, 1)
            o_a = _flash_t(r3(qa), r3(kab), vat, dl, sg_col, diff=True, half=hd_a, lam_init=lam_init)
            o_b = _flash_t(r3(qm), r3(km), vmt, dl, sg_col, diff=False, half=v_b, lam_init=lam_init)
            xp = _att_merge(xp, o_a.reshape(n_p, wa), o_b.reshape(n_p, h_b * v_b), w_oa, w_ob)
            outs_p["ak"].append(kaf.reshape(bp, sp, h_a, hd2))
            outs_p["av"].append(vaf.reshape(bp, sp, h_a, hd2))
            outs_p["lat"].append(c.reshape(bp, sp, kv_lora))
            outs_p["kr"].append(kr.reshape(bp, sp, rope_b))

            qa, kaf, kab, vaf, vab, _, qm, c, kr, _, _ = proj(
                xs, norm_g[layer, 1], tabs_a=_rope_tables_a(pos_s_rows, hd_a, rot_a),
                tabs_b=_rope_tables_b(pos_s_rows, nope_b, rope_b), tab_index=idx_s)
            q5 = qa.reshape(bs, ts, h_a, 2, hd_a).transpose(0, 2, 3, 1, 4)
            eye = jnp.eye(2, dtype=BF16)[None, None, :, None, :, None]
            wq = (q5[:, :, :, :, None, :] * eye).reshape(bs, h_a, 2 * ts, hd2)
            new_keys = -(-ts // 16) * 16
            pad_new = lambda z: jnp.pad(z.reshape(bs, ts, h_a, hd2).transpose(0, 2, 1, 3),
                                        ((0, 0), (0, 0), (0, new_keys - ts), (0, 0)))
            o_a = _diff_dec(page_table, dl, sg, wq, pad_new(kab), pad_new(vab), cache_k4, cache_v4, j,
                            t=ts, lam_init=lam_init)
            o_a = o_a.transpose(0, 2, 1, 3).reshape(n_s, wa).astype(BF16)

            w_abs = jnp.zeros((h_b, LANES, kv_lora), F32).at[:, :nope_b].set(mla_w_uk[j].transpose(1, 2, 0))
            q_lat = _head_mm(qm, w_abs.astype(BF16), BF16)
            q_lat = q_lat.reshape(bs, ts, h_b, kv_lora).transpose(0, 2, 1, 3).reshape(bs, h_b * ts, kv_lora)
            q_r = qm.reshape(bs, ts, h_b, LANES)[..., nope_b:nope_b + rope_b]
            q_r = q_r.transpose(0, 2, 1, 3).reshape(bs, h_b * ts, rope_b)
            new_keys = -(-ts // 16) * 16
            pad_keys = lambda z: jnp.pad(z.reshape(bs, ts, -1).astype(BF16), ((0, 0), (0, new_keys - ts), (0, 0)))
            o_lat = _mla_dec(page_table, q_lat, q_r, pad_keys(c), pad_keys(kr), cache_b_lat,
                             jnp.swapaxes(cache_b_rope, 2, 3), j, t=ts)
            o_lat = o_lat.reshape(bs, h_b, ts, kv_lora).transpose(0, 2, 1, 3).reshape(n_s, h_b * kv_lora)
            o_b = _head_mm(o_lat.astype(BF16), mla_w_uv[j].transpose(1, 0, 2).astype(BF16), BF16)
            xs = _att_merge(xs, o_a, o_b, w_oa, w_ob)
            outs_s["ak"].append(kaf.reshape(bs, ts, h_a, hd2))
            outs_s["av"].append(vaf.reshape(bs, ts, h_a, hd2))
            outs_s["lat"].append(c.reshape(bs, ts, kv_lora))
            outs_s["kr"].append(kr.reshape(bs, ts, rope_b))
        else:
            w_in = ret_w_in[j].astype(BF16)
            w_out = ret_w_out[j].astype(BF16)
            inv = 1.0 / (RET_THETA ** (jnp.arange(0, dk_r, 2, dtype=F32) / dk_r))

            def tables(pos):
                ang = pos.astype(F32)[:, None] * inv[None, :]
                return jnp.cos(ang), jnp.sin(ang)

            cos, sin = tables(pos_p)
            q, k, v, g = _ret_proj(xp, norm_g[layer, 1], w_in, cos, sin, idx_p, n_h=h_r, dk=dk_r, dv=dv_r)
            cp = RET_CHUNK if sp % RET_CHUNK == 0 else sp
            r3 = lambda z: z.reshape(bp, sp, z.shape[1])
            o, s_fin = _retention(r3(q), r3(k), r3(v), _ret_tables(h_r, cp, cp), ret_gn_g[j], None, n_h=h_r, c=cp)
            xp = _ret_merge(xp, o.reshape(n_p, h_r * dv_r), g, w_out)
            outs_p["ret"].append(s_fin)

            cos, sin = tables(pos_s_rows)
            q, k, v, g = _ret_proj(xs, norm_g[layer, 1], w_in, cos, sin, idx_s, n_h=h_r, dk=dk_r, dv=dv_r)
            cs = RET_CHUNK if ts % RET_CHUNK == 0 else ts
            rows = -(-cs // 16) * 16
            tpad = (ts // cs) * rows
            padt = lambda z: jnp.pad(z.reshape(bs, ts // cs, cs, z.shape[1]),
                                     ((0, 0), (0, 0), (0, rows - cs), (0, 0))).reshape(bs, tpad, z.shape[1])
            o, s_fin = _retention(padt(q), padt(k), padt(v), _ret_tables(h_r, cs, rows), ret_gn_g[j],
                                  state_ret[j], n_h=h_r, c=rows)
            o = o.reshape(bs, ts // cs, rows, h_r * dv_r)[:, :, :cs].reshape(n_s, h_r * dv_r)
            xs = _ret_merge(xs, o, g, w_out)
            outs_s["ret"].append(s_fin)
        w2 = _ffn_weights(ffn_w_gate[layer, 1], ffn_w_up[layer, 1], ffn_w_down[layer, 1])
        fin = final_norm_g if layer == depth - 1 else None
        xp = _ffn(xp, norm_g[layer, 2], w2, fin)
        xs = _ffn(xs, norm_g[layer, 2], w2, fin)

    return (xp.reshape(bp, sp, d), xs.reshape(bs, ts, d),
            jnp.stack(outs_p["ak"]), jnp.stack(outs_p["av"]), jnp.stack(outs_p["lat"]), jnp.stack(outs_p["kr"]),
            jnp.stack(outs_p["ret"]),
            jnp.stack(outs_s["ak"]), jnp.stack(outs_s["av"]), jnp.stack(outs_s["lat"]), jnp.stack(outs_s["kr"]),
            jnp.stack(outs_s["ret"]))
```

```python
import functools
import math

import numpy as np
import jax
import jax.numpy as jnp
from jax import lax
from jax.experimental import pallas as pl
from jax.experimental.pallas import tpu as pltpu

F32 = jnp.float32
BF16 = jnp.bfloat16

RMS_EPS = 1e-6
NEG_INF = -1e30
ROPE_THETA = 500000.0
RET_THETA = 10000.0
RET_CHUNK = 128
LOG2E = math.log2(math.e)

LANES = 128
VMEM_LIMIT = 56 * 1024 * 1024


def _cparams(*sem):
    return pltpu.CompilerParams(dimension_semantics=sem, vmem_limit_bytes=VMEM_LIMIT)


def _rms(x, g):
    return x * lax.rsqrt(jnp.mean(x * x, axis=-1, keepdims=True) + RMS_EPS) * g


def _dot(a, b):
    return jnp.dot(a, b, preferred_element_type=F32)


def _dot_nt(a, b):
    return lax.dot_general(a, b, (((1,), (1,)), ((), ())), preferred_element_type=F32)


def _dot_tn(a, b):
    return lax.dot_general(a, b, (((0,), (0,)), ((), ())), preferred_element_type=F32)


def _row_tile(n):
    for t in (512, 256, 128, 64, 32, 16, 8):
        if n % t == 0:
            return t
    raise ValueError(f"row count {n} is not a multiple of 8")


def _resident(shape):
    nd = len(shape)
    return pl.BlockSpec(shape, lambda *_: (0,) * nd, pipeline_mode=pl.Buffered(1))


def _rows(tm, width):
    return pl.BlockSpec((tm, width), lambda i: (i, 0))


def _ffn_body(*refs, tf, final):
    if final:
        x_ref, g_ref, wg_ref, wu_ref, wd_ref, fg_ref, o_ref, h_ref, act_ref = refs
    else:
        x_ref, g_ref, wg_ref, wu_ref, wd_ref, o_ref, h_ref, act_ref = refs
    x = x_ref[...]
    h_ref[...] = _rms(x, g_ref[...]).astype(BF16)
    for c0 in range(0, act_ref.shape[1], tf):
        h = h_ref[...]
        a = _dot(h, wg_ref[:, c0:c0 + tf])
        u = _dot(h, wu_ref[:, c0:c0 + tf])
        act_ref[:, c0:c0 + tf] = (a * jax.nn.sigmoid(a) * u).astype(BF16)
    y = x + 0.5 * _dot(act_ref[...], wd_ref[...])
    if final:
        y = _rms(y, fg_ref[...])
    o_ref[...] = y


def _ffn(x, g, w, layer, idx, final_g=None):
    n, d = x.shape
    wg, wu, wd = w
    dff = wg.shape[3]
    tf = next(t for t in (512, 256, 128) if dff % t == 0)
    tm = _row_tile(n)
    final = final_g is not None
    ins = [x, g.reshape(1, d), wg, wu, wd]
    pick = lambda a: pl.BlockSpec((None, None) + a.shape[2:], lambda i: (layer, idx, 0, 0),
                                  pipeline_mode=pl.Buffered(1))
    specs = [_rows(tm, d), _resident((1, d)), pick(wg), pick(wu), pick(wd)]
    if final:
        ins.append(final_g.reshape(1, d))
        specs.append(_resident((1, d)))
    return pl.pallas_call(
        functools.partial(_ffn_body, tf=tf, final=final),
        grid=(n // tm,),
        in_specs=specs,
        out_specs=_rows(tm, d),
        out_shape=jax.ShapeDtypeStruct((n, d), F32),
        scratch_shapes=[pltpu.VMEM((tm, d), BF16), pltpu.VMEM((tm, dff), BF16)],
        compiler_params=_cparams("parallel"),
        name="ffn_half",
    )(*ins)


def _rope_lanes(z, c, s_up, s_dn, shift):
    return z * c + pltpu.roll(z, LANES - shift, 1) * s_up + pltpu.roll(z, shift, 1) * s_dn


def _rope_tables_a(pos, hd, rot):
    half = rot // 2
    lane = np.arange(LANES) % hd
    idx = np.where(lane < rot, lane % half, 0)
    inv = 1.0 / (ROPE_THETA ** (jnp.arange(0, rot, 2, dtype=F32) / rot))
    ang = pos.astype(F32)[:, None] * inv[None, :]
    cos = jnp.cos(ang)[:, idx]
    sin = jnp.sin(ang)[:, idx]
    first = jnp.asarray(lane < half)[None, :]
    second = jnp.asarray((lane >= half) & (lane < rot))[None, :]
    c = jnp.where(first | second, cos, 1.0)
    s_up = jnp.where(first, -sin, 0.0)
    s_dn = jnp.where(second, sin, 0.0)
    return c, s_up, s_dn


def _rope_tables_b(pos, off, dim):
    half = dim // 2
    lane = np.arange(LANES)
    rel = lane - off
    idx = np.where((rel >= 0) & (rel < dim), rel % half, 0)
    inv = 1.0 / (ROPE_THETA ** (jnp.arange(0, dim, 2, dtype=F32) / dim))
    ang = pos.astype(F32)[:, None] * inv[None, :]
    cos = jnp.cos(ang)[:, idx]
    sin = jnp.sin(ang)[:, idx]
    first = jnp.asarray((rel >= 0) & (rel < half))[None, :]
    second = jnp.asarray((rel >= half) & (rel < dim))[None, :]
    c = jnp.where(first | second, cos, 1.0)
    s_up = jnp.where(first, -sin, 0.0)
    s_dn = jnp.where(second, sin, 0.0)
    return c, s_up, s_dn


def _att_proj_body(x_ref, g_ref, win_ref, qng_ref, wuq_ref, kvg_ref, wuk_ref, wuv_ref,
                   ca_ref, ua_ref, da_ref, cb_ref, ub_ref, db_ref,
                   qa_o, kaf_o, kab_o, vaf_o, vab_o, vat_o, qm_o, c_o, kr_o, km_o, vmt_o,
                   *, wa, n_hb, rot_a, rope_b, rope_off, scale_a, scale_b):
    h = _rms(x_ref[...], g_ref[...]).astype(BF16)
    ca, ua, da = ca_ref[...], ua_ref[...], da_ref[...]
    cb, ub, db = cb_ref[...], ub_ref[...], db_ref[...]
    sh_a = rot_a // 2
    sh_b = rope_b // 2

    n_ha = wa // LANES
    tm = x_ref.shape[0]
    zq = _dot(h, win_ref[:, 0:wa])
    zk = _dot(h, win_ref[:, wa:2 * wa])
    for j in range(n_ha):
        sl = slice(j * LANES, (j + 1) * LANES)
        qa_o[:, sl] = (_rope_lanes(zq[:, sl], ca, ua, da, sh_a) * scale_a).astype(BF16)
        kr = _rope_lanes(zk[:, sl], ca, ua, da, sh_a)
        kab_o[:, sl] = kr.astype(BF16)
        kaf_o[pl.ds(j, tm, stride=n_ha), :] = kr
    zv = _dot(h, win_ref[:, 2 * wa:3 * wa])
    for j in range(n_ha):
        vaf_o[pl.ds(j, tm, stride=n_ha), :] = zv[:, j * LANES:(j + 1) * LANES]
    vab_o[...] = zv.astype(BF16)
    vat_o[...] = zv.T.astype(BF16)

    o = 3 * wa
    ql = qng_ref.shape[1]
    cq = _rms(_dot(h, win_ref[:, o:o + ql]), qng_ref[...]).astype(BF16)
    qb = _dot(cq, wuq_ref[...])
    o += ql
    kvl = kvg_ref.shape[1]
    c = _rms(_dot(h, win_ref[:, o:o + kvl]), kvg_ref[...])
    c_o[...] = c
    cbf = c.astype(BF16)
    o += kvl
    kr = _rope_lanes(_dot(h, win_ref[:, o:o + LANES]), cb, ub, db, sh_b)
    kr_o[...] = kr[:, rope_off:rope_off + rope_b]
    kn = _dot(cbf, wuk_ref[...])
    for j in range(n_hb):
        sl = slice(j * LANES, (j + 1) * LANES)
        qm_o[:, sl] = (_rope_lanes(qb[:, sl], cb, ub, db, sh_b) * scale_b).astype(BF16)
        km_o[:, sl] = (kn[:, sl] + kr).astype(BF16)
    vmt_o[...] = _dot(cbf, wuv_ref[...]).T.astype(BF16)


def _att_weights(w_in, w_uq, w_uk, w_uv, wa, q_lora, kv_lora, rope_b, n_hb, nope_b, v_b):
    d = w_in.shape[0]
    o = 3 * wa + q_lora + kv_lora
    kr_cols = jnp.zeros((d, LANES), F32).at[:, nope_b:nope_b + rope_b].set(w_in[:, o:o + rope_b])
    win_p = jnp.concatenate([w_in[:, :o], kr_cols], axis=1).astype(BF16)
    hq = nope_b + rope_b
    wuq_p = jnp.zeros((q_lora, n_hb, LANES), F32).at[:, :, :hq].set(w_uq.reshape(q_lora, n_hb, hq))
    wuq_p = wuq_p.reshape(q_lora, n_hb * LANES).astype(BF16)
    wuk_p = jnp.zeros((kv_lora, n_hb, LANES), F32).at[:, :, :nope_b].set(w_uk)
    wuk_p = wuk_p.reshape(kv_lora, n_hb * LANES).astype(BF16)
    wuv_p = w_uv.reshape(kv_lora, n_hb * v_b).astype(BF16)
    return win_p, wuq_p, wuk_p, wuv_p


def _att_proj(x, g, weights, qng, kvg, tabs_a, tabs_b, tab_index, *, wa, n_hb, rot_a, rope_b, nope_b, v_b,
              scale_a, scale_b):
    n, d = x.shape
    win_p, wuq_p, wuk_p, wuv_p = weights
    tm = _row_tile(n)
    tab_spec = pl.BlockSpec((tm, LANES), lambda i: (tab_index(i), 0))
    kv_lora = kvg.shape[0]
    row_out = lambda w, dt: (_rows(tm, w), jax.ShapeDtypeStruct((n, w), dt))
    t_out = lambda w: (pl.BlockSpec((None, w, tm), lambda i: (i, 0, 0)), jax.ShapeDtypeStruct((n // tm, w, tm), BF16))
    n_ha = wa // LANES
    flat_out = (_rows(tm * n_ha, LANES), jax.ShapeDtypeStruct((n * n_ha, LANES), F32))
    outs = [row_out(wa, BF16), flat_out, row_out(wa, BF16), flat_out, row_out(wa, BF16), t_out(wa),
            row_out(n_hb * LANES, BF16), row_out(kv_lora, F32), row_out(rope_b, F32), row_out(n_hb * LANES, BF16),
            t_out(n_hb * v_b)]
    return pl.pallas_call(
        functools.partial(_att_proj_body, wa=wa, n_hb=n_hb, rot_a=rot_a, rope_b=rope_b, rope_off=nope_b,
                          scale_a=scale_a, scale_b=scale_b),
        grid=(n // tm,),
        in_specs=[_rows(tm, d), _resident((1, d)), _resident(win_p.shape), _resident((1, qng.shape[0])),
                  _resident(wuq_p.shape), _resident((1, kv_lora)), _resident(wuk_p.shape),
                  _resident(wuv_p.shape)] + [tab_spec] * 6,
        out_specs=[spec for spec, _ in outs],
        out_shape=[shape for _, shape in outs],
        compiler_params=_cparams("parallel"),
        name="att_proj",
    )(x, g.reshape(1, d), win_p, qng.reshape(1, -1), wuq_p, kvg.reshape(1, -1), wuk_p, wuv_p,
      *tabs_a, *tabs_b)


def _lam_of(dl, lam_init):
    s01 = jnp.sum(dl[0:1] * dl[1:2], axis=-1, keepdims=True)
    s23 = jnp.sum(dl[2:3] * dl[3:4], axis=-1, keepdims=True)
    return jnp.exp(s01) - jnp.exp(s23) + lam_init


FLASH_TQ = 512
FLASH_TK = 512


def _flash_t_body(dl_ref, g_ref, q_ref, k_ref, vt_ref, o_ref, *, diff, nj, tq, tk, half, lam_init):
    qi = pl.program_id(2)
    qw = LANES if diff else 2 * LANES
    lane = lax.broadcasted_iota(jnp.int32, (tq, LANES), 1)
    qs = []
    for j in range(nj):
        q = q_ref[:, j * qw:(j + 1) * qw]
        if diff:
            qs += [jnp.where(lane < half, q, jnp.zeros_like(q)), jnp.where(lane >= half, q, jnp.zeros_like(q))]
        else:
            qs += [q[:, :LANES], q[:, LANES:]]
    nc = 2 * nj
    vrows = LANES if diff else half
    q0 = qi * tq
    n_full = q0 // tk

    def step(ki, carry, masked):
        ks = pl.multiple_of(ki * tk, tk)
        ss = []
        for c in range(nc):
            j, a = divmod(c, 2)
            k0 = j * qw + (0 if diff else a * LANES)
            s = _dot_nt(k_ref[pl.ds(ks, tk), k0:k0 + LANES], qs[c])
            if masked:
                key = lax.broadcasted_iota(jnp.int32, (tk, tq), 0) + ks
                qry = lax.broadcasted_iota(jnp.int32, (tk, tq), 1) + q0
                s = jnp.where(key <= qry, s, NEG_INF)
            ss.append(s)
        out = []
        for c in range(nc):
            j, a = divmod(c, 2)
            m_old, l_old, acc = carry[c]
            m_new = jnp.maximum(m_old, jnp.max(ss[c], axis=0, keepdims=True))
            corr = jnp.exp2(m_old - m_new)
            p = jnp.exp2(ss[c] - m_new)
            l_new = l_old * corr + jnp.sum(p, axis=0, keepdims=True)
            r0 = j * LANES + (0 if diff else a * half)
            acc = acc * corr + _dot(vt_ref[ki, r0:r0 + vrows, :], p.astype(BF16))
            out.append((m_new, l_new, acc))
        return tuple(out)

    init = tuple((jnp.full((1, tq), NEG_INF, F32), jnp.zeros((1, tq), F32), jnp.zeros((vrows, tq), F32))
                 for _ in range(nc))
    carry = lax.fori_loop(0, n_full, lambda ki, cr: step(ki, cr, False), init)
    carry = step(n_full, carry, True)
    if diff:
        lam = _lam_of(dl_ref[...], lam_init)
    for j in range(nj):
        o0 = carry[2 * j][2] / carry[2 * j][1]
        o1 = carry[2 * j + 1][2] / carry[2 * j + 1][1]
        if diff:
            d = o0 - lam * o1
            ot = d * lax.rsqrt(jnp.mean(d * d, axis=0, keepdims=True) + RMS_EPS) * g_ref[...] * (1.0 - lam_init)
        else:
            ot = jnp.concatenate([o0, o1], axis=0)
        o_ref[:, j * LANES:(j + 1) * LANES] = ot.T.astype(o_ref.dtype)


def _flash_t(q, k, vt, dl, g_col, *, diff, half, lam_init):
    b, s, _ = q.shape
    tk = vt.shape[2]
    jt = vt.shape[1] // LANES
    nj = math.gcd(jt, 4)
    qw = LANES if diff else 2 * LANES
    tq = min(FLASH_TQ, tk)
    assert s % tk == 0 and tk % tq == 0
    return pl.pallas_call(
        functools.partial(_flash_t_body, diff=diff, nj=nj, tq=tq, tk=tk, half=half, lam_init=lam_init),
        grid=(b, jt // nj, s // tq),
        in_specs=[pl.BlockSpec(dl.shape, lambda bi, ji, qi: (0, 0)),
                  pl.BlockSpec(g_col.shape, lambda bi, ji, qi: (0, 0)),
                  pl.BlockSpec((None, tq, nj * qw), lambda bi, ji, qi: (bi, qi, ji)),
                  pl.BlockSpec((None, s, nj * qw), lambda bi, ji, qi: (bi, 0, ji), pipeline_mode=pl.Buffered(1)),
                  pl.BlockSpec((s // tk, nj * LANES, tk), lambda bi, ji, qi: (bi, ji, 0),
                               pipeline_mode=pl.Buffered(1))],
        out_specs=pl.BlockSpec((None, tq, nj * LANES), lambda bi, ji, qi: (bi, qi, ji)),
        out_shape=jax.ShapeDtypeStruct((b, s, jt * LANES), BF16),
        compiler_params=_cparams("parallel", "parallel", "arbitrary"),
        name="flash_diff" if diff else "flash_mla",
    )(dl, g_col, q, k, vt)


def _online(m_sc, l_sc, acc_sc, s, v):
    m_old = m_sc[...]
    m_new = jnp.maximum(m_old, jnp.max(s, axis=-1, keepdims=True))
    corr = jnp.exp2(m_old - m_new)
    p = jnp.exp2(s - m_new)
    l_sc[...] = l_sc[...] * corr + jnp.sum(p, axis=-1, keepdims=True)
    acc_sc[...] = acc_sc[...] * corr + _dot(p.astype(BF16), v)
    m_sc[...] = m_new


def _page_pipeline(pt_ref, caches, bufs, sems, layer, npg):
    ns = pl.num_programs(1)
    total = pl.num_programs(0) * ns
    g = pl.program_id(0) * ns + pl.program_id(1)
    slot = lax.rem(g, 2)

    def copy(c, page_id, slot_, i):
        return pltpu.make_async_copy(caches[c].at[layer, page_id], bufs[c].at[slot_, i], sems.at[slot_, i, c])

    def start(group, slot_):
        b = group // ns
        first = lax.rem(group, ns) * npg
        for i in range(npg):
            page_id = pt_ref[b, first + i]
            for c in range(len(caches)):
                copy(c, page_id, slot_, i).start()

    @pl.when(g == 0)
    def _():
        start(g, slot)

    @pl.when(g + 1 < total)
    def _():
        start(g + 1, 1 - slot)

    for i in range(npg):
        for c in range(len(caches)):
            copy(c, 0, slot, i).wait()
    return slot


def _pages_per_step(n_pages, cap):
    return next(gp for gp in range(min(cap, n_pages), 0, -1) if n_pages % gp == 0)


def _diff_dec_body(pt_ref, dl_ref, g_ref, wq_ref, kn_ref, vn_ref, ck_hbm, cv_hbm, o_ref,
                   kbuf, vbuf, sems, m_sc, l_sc, acc_sc, *, layer, npg, n_h, t, lam_init):
    slot = _page_pipeline(pt_ref, (ck_hbm, cv_hbm), (kbuf, vbuf), sems, layer, npg)
    st = pl.program_id(1)
    page = kbuf.shape[2] // n_h

    @pl.when(st == 0)
    def _():
        m_sc[...] = jnp.full_like(m_sc, NEG_INF)
        l_sc[...] = jnp.zeros_like(l_sc)
        acc_sc[...] = jnp.zeros_like(acc_sc)

    def head_rows(buf, h):
        return jnp.concatenate([buf[slot, i, pl.ds(h, page, stride=n_h), :].astype(BF16) for i in range(npg)], axis=0)

    m_old, l_old, acc_old = m_sc[...], l_sc[...], acc_sc[...]
    ss = [_dot_nt(wq_ref[h], head_rows(kbuf, h)) for h in range(n_h)]
    m_new = [jnp.maximum(m_old[h], jnp.max(ss[h], axis=-1, keepdims=True)) for h in range(n_h)]
    ps = [jnp.exp2(ss[h] - m_new[h]) for h in range(n_h)]
    pv = [_dot(ps[h].astype(BF16), head_rows(vbuf, h)) for h in range(n_h)]
    for h in range(n_h):
        corr = jnp.exp2(m_old[h] - m_new[h])
        m_sc[h] = m_new[h]
        l_sc[h] = l_old[h] * corr + jnp.sum(ps[h], axis=-1, keepdims=True)
        acc_sc[h] = acc_old[h] * corr + pv[h]

    @pl.when(st == pl.num_programs(1) - 1)
    def _():
        for h in range(n_h):
            kn = kn_ref[h]
            row = lax.broadcasted_iota(jnp.int32, (2 * t, kn.shape[0]), 0)
            col = lax.broadcasted_iota(jnp.int32, (2 * t, kn.shape[0]), 1)
            s = jnp.where(col <= lax.rem(row, t), _dot_nt(wq_ref[h], kn), NEG_INF)
            _online(m_sc.at[h], l_sc.at[h], acc_sc.at[h], s, vn_ref[h])
        o = acc_sc[...] / l_sc[...]
        lam = _lam_of(dl_ref[...], lam_init)
        o_ref[...] = _rms(o[:, :t] - lam * o[:, t:], g_ref[...]) * (1.0 - lam_init)


def _diff_dec(page_table, dl, g, wq, kn, vn, cache_k, cache_v, layer, *, t, lam_init):
    nb, n_pages = page_table.shape
    npg = _pages_per_step(n_pages, 16)
    n_h = wq.shape[1]
    prow = cache_k.shape[2]
    per_b = lambda shape: pl.BlockSpec((None,) + shape, lambda b, s, pt: (b,) + (0,) * len(shape))
    const = lambda shape: pl.BlockSpec(shape, lambda b, s, pt: (0,) * len(shape))
    hbm = pl.BlockSpec(memory_space=pl.ANY)
    grid_spec = pltpu.PrefetchScalarGridSpec(
        num_scalar_prefetch=1,
        grid=(nb, n_pages // npg),
        in_specs=[const(dl.shape), const(g.shape), per_b(wq.shape[1:]), per_b(kn.shape[1:]), per_b(vn.shape[1:]),
                  hbm, hbm],
        out_specs=per_b((n_h, t, LANES)),
        scratch_shapes=[pltpu.VMEM((2, npg, prow, LANES), F32), pltpu.VMEM((2, npg, prow, LANES), F32),
                        pltpu.SemaphoreType.DMA((2, npg, 2)),
                        pltpu.VMEM((n_h, 2 * t, 1), F32), pltpu.VMEM((n_h, 2 * t, 1), F32),
                        pltpu.VMEM((n_h, 2 * t, LANES), F32)],
    )
    return pl.pallas_call(
        functools.partial(_diff_dec_body, layer=layer, npg=npg, n_h=n_h, t=t, lam_init=lam_init),
        grid_spec=grid_spec,
        out_shape=jax.ShapeDtypeStruct((nb, n_h, t, LANES), F32),
        compiler_params=_cparams("arbitrary", "arbitrary"),
        name="diff_decode",
    )(page_table, dl, g, wq, kn, vn, cache_k, cache_v)


def _mla_dec_body(pt_ref, ql_ref, qr_ref, cn_ref, rn_ref, lat_hbm, rope_hbm, o_ref,
                  lbuf, rbuf, sems, m_sc, l_sc, acc_sc, *, layer, npg, t):
    slot = _page_pipeline(pt_ref, (lat_hbm, rope_hbm), (lbuf, rbuf), sems, layer, npg)
    st = pl.program_id(1)
    nrow = ql_ref.shape[0]

    @pl.when(st == 0)
    def _():
        m_sc[...] = jnp.full_like(m_sc, NEG_INF)
        l_sc[...] = jnp.zeros_like(l_sc)
        acc_sc[...] = jnp.zeros_like(acc_sc)

    ql = ql_ref[...]
    qr = qr_ref[...]
    lat = jnp.concatenate([lbuf[slot, i].astype(BF16) for i in range(npg)], axis=0)
    rope_t = jnp.concatenate([rbuf[slot, i].astype(BF16) for i in range(npg)], axis=1)
    _online(m_sc, l_sc, acc_sc, _dot_nt(ql, lat) + _dot(qr, rope_t), lat)

    @pl.when(st == pl.num_programs(1) - 1)
    def _():
        cn = cn_ref[...]
        ncol = cn.shape[0]
        row = lax.broadcasted_iota(jnp.int32, (nrow, ncol), 0)
        col = lax.broadcasted_iota(jnp.int32, (nrow, ncol), 1)
        s = jnp.where(col <= (row % t), _dot_nt(ql, cn) + _dot_nt(qr, rn_ref[...]), NEG_INF)
        _online(m_sc, l_sc, acc_sc, s, cn)
        o_ref[...] = acc_sc[...] / l_sc[...]


def _mla_dec(page_table, ql, qr, cn, rn, cache_lat, cache_rope_t, layer, *, t):
    nb, n_pages = page_table.shape
    npg = _pages_per_step(n_pages, 64)
    nrow, r = ql.shape[1:]
    e = qr.shape[2]
    page = cache_lat.shape[2]
    per_b = lambda shape: pl.BlockSpec((None,) + shape, lambda b, s, pt: (b, 0, 0))
    hbm = pl.BlockSpec(memory_space=pl.ANY)
    grid_spec = pltpu.PrefetchScalarGridSpec(
        num_scalar_prefetch=1,
        grid=(nb, n_pages // npg),
        in_specs=[per_b((nrow, r)), per_b((nrow, e)), per_b(cn.shape[1:]), per_b(rn.shape[1:]), hbm, hbm],
        out_specs=per_b((nrow, r)),
        scratch_shapes=[pltpu.VMEM((2, npg, page, r), F32), pltpu.VMEM((2, npg, e, page), F32),
                        pltpu.SemaphoreType.DMA((2, npg, 2)),
                        pltpu.VMEM((nrow, 1), F32), pltpu.VMEM((nrow, 1), F32), pltpu.VMEM((nrow, r), F32)],
    )
    return pl.pallas_call(
        functools.partial(_mla_dec_body, layer=layer, npg=npg, t=t),
        grid_spec=grid_spec,
        out_shape=jax.ShapeDtypeStruct((nb, nrow, r), F32),
        compiler_params=_cparams("arbitrary", "arbitrary"),
        name="mla_decode",
    )(page_table, ql, qr, cn, rn, cache_lat, cache_rope_t)


def _head_mm_body(x_ref, w_ref, o_ref, *, n_h, wi, wo):
    for h in range(n_h):
        o_ref[:, h * wo:(h + 1) * wo] = _dot(x_ref[:, h * wi:(h + 1) * wi], w_ref[h]).astype(o_ref.dtype)


def _head_mm(x, w, out_dtype):
    n = x.shape[0]
    n_h, wi, wo = w.shape
    return pl.pallas_call(
        functools.partial(_head_mm_body, n_h=n_h, wi=wi, wo=wo),
        out_shape=jax.ShapeDtypeStruct((n, n_h * wo), out_dtype),
        compiler_params=pltpu.CompilerParams(vmem_limit_bytes=VMEM_LIMIT),
        name="head_matmul",
    )(x, w)


def _att_merge_body(x_ref, oa_ref, ob_ref, wa_ref, wb_ref, o_ref):
    o_ref[...] = x_ref[...] + _dot(oa_ref[...], wa_ref[...]) + _dot(ob_ref[...], wb_ref[...])


def _att_merge(x, oa, ob, wa, wb):
    n, d = x.shape
    tm = _row_tile(n)
    return pl.pallas_call(
        _att_merge_body,
        grid=(n // tm,),
        in_specs=[_rows(tm, d), _rows(tm, oa.shape[1]), _rows(tm, ob.shape[1]), _resident(wa.shape),
                  _resident(wb.shape)],
        out_specs=_rows(tm, d),
        out_shape=jax.ShapeDtypeStruct((n, d), F32),
        compiler_params=_cparams("parallel"),
        name="att_merge",
    )(x, oa, ob, wa, wb)


def _ret_merge_body(x_ref, y_ref, g_ref, w_ref, o_ref):
    g = g_ref[...]
    z = (y_ref[...].astype(F32) * (g * jax.nn.sigmoid(g))).astype(BF16)
    o_ref[...] = x_ref[...] + _dot(z, w_ref[...])


def _ret_merge(x, y, g, w):
    n, d = x.shape
    tm = _row_tile(n)
    return pl.pallas_call(
        _ret_merge_body,
        grid=(n // tm,),
        in_specs=[_rows(tm, d), _rows(tm, y.shape[1]), _rows(tm, g.shape[1]), _resident(w.shape)],
        out_specs=_rows(tm, d),
        out_shape=jax.ShapeDtypeStruct((n, d), F32),
        compiler_params=_cparams("parallel"),
        name="ret_merge",
    )(x, y, g, w)


def _ret_proj_body(x_ref, g_ref, w_ref, cos_ref, sin_ref, q_o, k_o, v_o, g_o, *, n_h, dk, dv, k_scale):
    h = _rms(x_ref[...], g_ref[...]).astype(BF16)
    cos, sin = cos_ref[...], sin_ref[...]
    half = dk // 2
    wq = n_h * dk
    for part, (out, scale) in enumerate(((q_o, 1.0), (k_o, k_scale))):
        z = _dot(h, w_ref[:, part * wq:(part + 1) * wq])
        for j in range(n_h):
            for c0 in range(0, half, LANES):
                a = slice(j * dk + c0, j * dk + c0 + LANES)
                b = slice(j * dk + half + c0, j * dk + half + c0 + LANES)
                cs, sn = cos[:, c0:c0 + LANES], sin[:, c0:c0 + LANES]
                z1, z2 = z[:, a], z[:, b]
                out[:, a] = ((z1 * cs - z2 * sn) * scale).astype(BF16)
                out[:, b] = ((z1 * sn + z2 * cs) * scale).astype(BF16)
    wv = n_h * dv
    v_o[...] = _dot(h, w_ref[:, 2 * wq:2 * wq + wv]).astype(BF16)
    g_o[...] = _dot(h, w_ref[:, 2 * wq + wv:2 * wq + 2 * wv])


def _ret_proj(x, g, w, cos, sin, tab_index, *, n_h, dk, dv):
    n, d = x.shape
    tm = _row_tile(n)
    half = dk // 2
    tab_spec = pl.BlockSpec((tm, half), lambda i: (tab_index(i), 0))
    out_w = [(n_h * dk, BF16), (n_h * dk, BF16), (n_h * dv, BF16), (n_h * dv, F32)]
    return pl.pallas_call(
        functools.partial(_ret_proj_body, n_h=n_h, dk=dk, dv=dv, k_scale=dk ** -0.5),
        grid=(n // tm,),
        in_specs=[_rows(tm, d), _resident((1, d)), _resident(w.shape), tab_spec, tab_spec],
        out_specs=[_rows(tm, wd) for wd, _ in out_w],
        out_shape=[jax.ShapeDtypeStruct((n, wd), dt) for wd, dt in out_w],
        compiler_params=_cparams("parallel"),
        name="ret_proj",
    )(x, g.reshape(1, d), w, cos, sin)


def _ret_body(*refs, has_s0):
    if has_s0:
        q_ref, k_ref, v_ref, in_ref, qd_ref, kd_ref, cd_ref, gn_ref, s0_ref, o_ref, st_ref = refs
    else:
        q_ref, k_ref, v_ref, in_ref, qd_ref, kd_ref, cd_ref, gn_ref, o_ref, st_ref = refs
    c = pl.program_id(1)
    n_h, dk, dv = st_ref.shape

    @pl.when(c == 0)
    def _():
        st_ref[...] = s0_ref[...] if has_s0 else jnp.zeros_like(st_ref)

    sts = [st_ref[h] for h in range(n_h)]
    new = []
    for h in range(n_h):
        q, k = q_ref[:, h * dk:(h + 1) * dk], k_ref[:, h * dk:(h + 1) * dk]
        v = v_ref[:, h * dv:(h + 1) * dv]
        att = (_dot_nt(q, k) * in_ref[h]).astype(BF16)
        o = _dot(att, v) + _dot(q, sts[h].astype(BF16)) * qd_ref[h]
        kd = (k.astype(F32) * kd_ref[h]).astype(BF16)
        new.append(sts[h] * cd_ref[h] + _dot_tn(kd, v))
        o_ref[:, h * dv:(h + 1) * dv] = _rms(o, gn_ref[...]).astype(o_ref.dtype)
    for h in range(n_h):
        st_ref[h] = new[h]


def _ret_tables(n_h, c, rows):
    log_g = jnp.log1p(-jnp.exp2(-5.0 - jnp.arange(n_h, dtype=F32)))
    idx = jnp.arange(c, dtype=F32)
    dif = idx[:, None] - idx[None, :]
    intra = jnp.where(dif >= 0, jnp.exp(jnp.maximum(dif, 0.0)[None] * log_g[:, None, None]), 0.0)
    q_dec = jnp.exp((idx + 1.0)[None] * log_g[:, None])[..., None]
    k_dec = jnp.exp((c - 1.0 - idx)[None] * log_g[:, None])[..., None]
    c_dec = jnp.exp(c * log_g)[:, None, None]
    p = rows - c
    return (jnp.pad(intra, ((0, 0), (0, p), (0, p))), jnp.pad(q_dec, ((0, 0), (0, p), (0, 0))),
            jnp.pad(k_dec, ((0, 0), (0, p), (0, 0))), c_dec)


def _retention(q, k, v, tables, gn, s0, *, n_h, c):
    b, t, _ = q.shape
    dk = q.shape[2] // n_h
    dv = v.shape[2] // n_h
    intra, q_dec, k_dec, c_dec = tables
    blk = lambda w: pl.BlockSpec((None, c, n_h * w), lambda bi, ci: (bi, ci, 0))
    table = lambda shape: pl.BlockSpec(shape, lambda bi, ci: (0,) * len(shape))
    st_spec = pl.BlockSpec((None, n_h, dk, dv), lambda bi, ci: (bi, 0, 0, 0))
    ins = [q, k, v, intra, q_dec, k_dec, c_dec, gn.reshape(1, dv)]
    specs = [blk(dk), blk(dk), blk(dv), table(intra.shape), table(q_dec.shape), table(k_dec.shape),
             table(c_dec.shape), table((1, dv))]
    if s0 is not None:
        ins.append(s0)
        specs.append(st_spec)
    return pl.pallas_call(
        functools.partial(_ret_body, has_s0=s0 is not None),
        grid=(b, t // c),
        in_specs=specs,
        out_specs=[blk(dv), st_spec],
        out_shape=[jax.ShapeDtypeStruct((b, t, n_h * dv), BF16), jax.ShapeDtypeStruct((b, n_h, dk, dv), F32)],
        compiler_params=_cparams("parallel", "arbitrary"),
        name="retention",
    )(*ins)


def kernel(x_prompt, x_sample, cache_a_k, cache_a_v, cache_b_lat, cache_b_rope, state_ret, page_table,
           norm_g, ffn_w_gate, ffn_w_up, ffn_w_down, att_w_in, diff_lambda, diff_subln_g,
           mla_q_norm_g, mla_w_uq, mla_kv_norm_g, mla_w_uk, mla_w_uv, att_w_out,
           ret_w_in, ret_gn_g, ret_w_out, final_norm_g):
    bp, sp, d = x_prompt.shape
    bs, ts, _ = x_sample.shape
    depth = norm_g.shape[0]
    n_layers_att, pool, page, h_a, hd2 = cache_a_k.shape
    hd_a = hd2 // 2
    rot_a = hd_a // 4
    wa = h_a * hd2
    kv_lora, h_b, nope_b = mla_w_uk.shape[1:]
    v_b = mla_w_uv.shape[3]
    rope_b = cache_b_rope.shape[3]
    q_lora = mla_q_norm_g.shape[1]
    h_r, dk_r, dv_r = state_ret.shape[2:]
    past = page_table.shape[1] * page
    n_p, n_s = bp * sp, bs * ts

    pos_p = jnp.arange(sp)
    pos_s = past + jnp.arange(ts)
    tm_p, tm_s = _row_tile(n_p), _row_tile(n_s)
    pos_s_rows = jnp.tile(pos_s, tm_s // ts)
    idx_p = lambda i: i % (sp // tm_p)
    idx_s = lambda i: 0

    xp = x_prompt.reshape(n_p, d)
    xs = x_sample.reshape(n_s, d)
    outs_p = {k: [] for k in ("ak", "av", "lat", "kr", "ret")}
    outs_s = {k: [] for k in ("ak", "av", "lat", "kr", "ret")}

    cache_k4 = cache_a_k.reshape(n_layers_att, pool, page * h_a, hd2)
    cache_v4 = cache_a_v.reshape(n_layers_att, pool, page * h_a, hd2)

    w_ffn = (ffn_w_gate.astype(BF16), ffn_w_up.astype(BF16), ffn_w_down.astype(BF16))
    for layer in range(depth):
        j = layer // 2
        xp = _ffn(xp, norm_g[layer, 0], w_ffn, layer, 0)
        xs = _ffn(xs, norm_g[layer, 0], w_ffn, layer, 0)
        if layer % 2 == 0:
            lam_init = 0.8 - 0.6 * math.exp(-0.3 * layer)
            scale_a = hd_a ** -0.5 * LOG2E
            scale_b = (nope_b + rope_b) ** -0.5 * LOG2E
            weights = _att_weights(att_w_in[j], mla_w_uq[j], mla_w_uk[j], mla_w_uv[j], wa, q_lora, kv_lora,
                                   rope_b, h_b, nope_b, v_b)
            proj = functools.partial(_att_proj, weights=weights, qng=mla_q_norm_g[j], kvg=mla_kv_norm_g[j],
                                     wa=wa, n_hb=h_b, rot_a=rot_a, rope_b=rope_b, nope_b=nope_b, v_b=v_b,
                                     scale_a=scale_a, scale_b=scale_b)
            dl = diff_lambda[j]
            sg = diff_subln_g[j].reshape(1, hd2)
            w_oa = att_w_out[j, :wa].astype(BF16)
            w_ob = att_w_out[j, wa:].astype(BF16)

            qa, kaf, kab, vaf, _, vat, qm, c, kr, km, vmt = proj(
                xp, norm_g[layer, 1], tabs_a=_rope_tables_a(pos_p, hd_a, rot_a),
                tabs_b=_rope_tables_b(pos_p, nope_b, rope_b), tab_index=idx_p)
            r3 = lambda z: z.reshape(bp, sp, z.shape[1])
            sg_col = diff_subln_g[j].reshape(hd2, 1)
            o_a = _flash_t(r3(qa), r3(kab), vat, dl, sg_col, diff=True, half=hd_a, lam_init=lam_init)
            o_b = _flash_t(r3(qm), r3(km), vmt, dl, sg_col, diff=False, half=v_b, lam_init=lam_init)
            xp = _att_merge(xp, o_a.reshape(n_p, wa), o_b.reshape(n_p, h_b * v_b), w_oa, w_ob)
            outs_p["ak"].append(kaf.reshape(bp, sp, h_a, hd2))
            outs_p["av"].append(vaf.reshape(bp, sp, h_a, hd2))
            outs_p["lat"].append(c.reshape(bp, sp, kv_lora))
            outs_p["kr"].append(kr.reshape(bp, sp, rope_b))

            qa, kaf, kab, vaf, vab, _, qm, c, kr, _, _ = proj(
                xs, norm_g[layer, 1], tabs_a=_rope_tables_a(pos_s_rows, hd_a, rot_a),
                tabs_b=_rope_tables_b(pos_s_rows, nope_b, rope_b), tab_index=idx_s)
            q5 = qa.reshape(bs, ts, h_a, 2, hd_a).transpose(0, 2, 3, 1, 4)
            eye = jnp.eye(2, dtype=BF16)[None, None, :, None, :, None]
            wq = (q5[:, :, :, :, None, :] * eye).reshape(bs, h_a, 2 * ts, hd2)
            new_keys = -(-ts // 16) * 16
            pad_new = lambda z: jnp.pad(z.reshape(bs, ts, h_a, hd2).transpose(0, 2, 1, 3),
                                        ((0, 0), (0, 0), (0, new_keys - ts), (0, 0)))
            o_a = _diff_dec(page_table, dl, sg, wq, pad_new(kab), pad_new(vab), cache_k4, cache_v4, j,
                            t=ts, lam_init=lam_init)
            o_a = o_a.transpose(0, 2, 1, 3).reshape(n_s, wa).astype(BF16)

            w_abs = jnp.zeros((h_b, LANES, kv_lora), F32).at[:, :nope_b].set(mla_w_uk[j].transpose(1, 2, 0))
            q_lat = _head_mm(qm, w_abs.astype(BF16), BF16)
            q_lat = q_lat.reshape(bs, ts, h_b, kv_lora).transpose(0, 2, 1, 3).reshape(bs, h_b * ts, kv_lora)
            q_r = qm.reshape(bs, ts, h_b, LANES)[..., nope_b:nope_b + rope_b]
            q_r = q_r.transpose(0, 2, 1, 3).reshape(bs, h_b * ts, rope_b)
            new_keys = -(-ts // 16) * 16
            pad_keys = lambda z: jnp.pad(z.reshape(bs, ts, -1).astype(BF16), ((0, 0), (0, new_keys - ts), (0, 0)))
            o_lat = _mla_dec(page_table, q_lat, q_r, pad_keys(c), pad_keys(kr), cache_b_lat,
                             jnp.swapaxes(cache_b_rope, 2, 3), j, t=ts)
            o_lat = o_lat.reshape(bs, h_b, ts, kv_lora).transpose(0, 2, 1, 3).reshape(n_s, h_b * kv_lora)
            o_b = _head_mm(o_lat.astype(BF16), mla_w_uv[j].transpose(1, 0, 2).astype(BF16), BF16)
            xs = _att_merge(xs, o_a, o_b, w_oa, w_ob)
            outs_s["ak"].append(kaf.reshape(bs, ts, h_a, hd2))
            outs_s["av"].append(vaf.reshape(bs, ts, h_a, hd2))
            outs_s["lat"].append(c.reshape(bs, ts, kv_lora))
            outs_s["kr"].append(kr.reshape(bs, ts, rope_b))
        else:
            w_in = ret_w_in[j].astype(BF16)
            w_out = ret_w_out[j].astype(BF16)
            inv = 1.0 / (RET_THETA ** (jnp.arange(0, dk_r, 2, dtype=F32) / dk_r))

            def tables(pos):
                ang = pos.astype(F32)[:, None] * inv[None, :]
                return jnp.cos(ang), jnp.sin(ang)

            cos, sin = tables(pos_p)
            q, k, v, g = _ret_proj(xp, norm_g[layer, 1], w_in, cos, sin, idx_p, n_h=h_r, dk=dk_r, dv=dv_r)
            cp = RET_CHUNK if sp % RET_CHUNK == 0 else sp
            r3 = lambda z: z.reshape(bp, sp, z.shape[1])
            o, s_fin = _retention(r3(q), r3(k), r3(v), _ret_tables(h_r, cp, cp), ret_gn_g[j], None, n_h=h_r, c=cp)
            xp = _ret_merge(xp, o.reshape(n_p, h_r * dv_r), g, w_out)
            outs_p["ret"].append(s_fin)

            cos, sin = tables(pos_s_rows)
            q, k, v, g = _ret_proj(xs, norm_g[layer, 1], w_in, cos, sin, idx_s, n_h=h_r, dk=dk_r, dv=dv_r)
            cs = RET_CHUNK if ts % RET_CHUNK == 0 else ts
            rows = -(-cs // 16) * 16
            tpad = (ts // cs) * rows
            padt = lambda z: jnp.pad(z.reshape(bs, ts // cs, cs, z.shape[1]),
                                     ((0, 0), (0, 0), (0, rows - cs), (0, 0))).reshape(bs, tpad, z.shape[1])
            o, s_fin = _retention(padt(q), padt(k), padt(v), _ret_tables(h_r, cs, rows), ret_gn_g[j],
                                  state_ret[j], n_h=h_r, c=rows)
            o = o.reshape(bs, ts // cs, rows, h_r * dv_r)[:, :, :cs].reshape(n_s, h_r * dv_r)
            xs = _ret_merge(xs, o, g, w_out)
            outs_s["ret"].append(s_fin)
        fin = final_norm_g if layer == depth - 1 else None
        xp = _ffn(xp, norm_g[layer, 2], w_ffn, layer, 1, fin)
        xs = _ffn(xs, norm_g[layer, 2], w_ffn, layer, 1, fin)

    return (xp.reshape(bp, sp, d), xs.reshape(bs, ts, d),
            jnp.stack(outs_p["ak"]), jnp.stack(outs_p["av"]), jnp.stack(outs_p["lat"]), jnp.stack(outs_p["kr"]),
            jnp.stack(outs_p["ret"]),
            jnp.stack(outs_s["ak"]), jnp.stack(outs_s["av"]), jnp.stack(outs_s["lat"]), jnp.stack(outs_s["kr"]),
            jnp.stack(outs_s["ret"]))
```

```python
import functools
import math

import numpy as np
import jax
import jax.numpy as jnp
from jax import lax
from jax.experimental import pallas as pl
from jax.experimental.pallas import tpu as pltpu

F32 = jnp.float32
BF16 = jnp.bfloat16

RMS_EPS = 1e-6
NEG_INF = -1e30
ROPE_THETA = 500000.0
RET_THETA = 10000.0
RET_CHUNK = 256
LOG2E = math.log2(math.e)

LANES = 128
VMEM_LIMIT = 56 * 1024 * 1024


def _cparams(*sem):
    return pltpu.CompilerParams(dimension_semantics=sem, vmem_limit_bytes=VMEM_LIMIT)


def _rms(x, g):
    return x * lax.rsqrt(jnp.mean(x * x, axis=-1, keepdims=True) + RMS_EPS) * g


def _dot(a, b):
    return jnp.dot(a, b, preferred_element_type=F32)


def _dot_nt(a, b):
    return lax.dot_general(a, b, (((1,), (1,)), ((), ())), preferred_element_type=F32)


def _dot_tn(a, b):
    return lax.dot_general(a, b, (((0,), (0,)), ((), ())), preferred_element_type=F32)


def _row_tile(n):
    for t in (512, 256, 128, 64, 32, 16, 8):
        if n % t == 0:
            return t
    raise ValueError(f"row count {n} is not a multiple of 8")


def _resident(shape):
    nd = len(shape)
    return pl.BlockSpec(shape, lambda *_: (0,) * nd, pipeline_mode=pl.Buffered(1))


def _rows(tm, width):
    return pl.BlockSpec((tm, width), lambda i: (i, 0))


def _ffn_body(*refs, tf, final):
    if final:
        x_ref, g_ref, wg_ref, wu_ref, wd_ref, fg_ref, o_ref, h_ref, act_ref = refs
    else:
        x_ref, g_ref, wg_ref, wu_ref, wd_ref, o_ref, h_ref, act_ref = refs
    x = x_ref[...]
    h_ref[...] = _rms(x, g_ref[...]).astype(BF16)
    for c0 in range(0, act_ref.shape[1], tf):
        h = h_ref[...]
        a = _dot(h, wg_ref[:, c0:c0 + tf])
        u = _dot(h, wu_ref[:, c0:c0 + tf])
        act_ref[:, c0:c0 + tf] = (a * jax.nn.sigmoid(a) * u).astype(BF16)
    y = x + 0.5 * _dot(act_ref[...], wd_ref[...])
    if final:
        y = _rms(y, fg_ref[...])
    o_ref[...] = y


def _ffn(x, g, w, layer, idx, final_g=None):
    n, d = x.shape
    wg, wu, wd = w
    dff = wg.shape[3]
    tf = next(t for t in (512, 256, 128) if dff % t == 0)
    tm = _row_tile(n)
    final = final_g is not None
    ins = [x, g.reshape(1, d), wg, wu, wd]
    pick = lambda a: pl.BlockSpec((None, None) + a.shape[2:], lambda i: (layer, idx, 0, 0),
                                  pipeline_mode=pl.Buffered(1))
    specs = [_rows(tm, d), _resident((1, d)), pick(wg), pick(wu), pick(wd)]
    if final:
        ins.append(final_g.reshape(1, d))
        specs.append(_resident((1, d)))
    return pl.pallas_call(
        functools.partial(_ffn_body, tf=tf, final=final),
        grid=(n // tm,),
        in_specs=specs,
        out_specs=_rows(tm, d),
        out_shape=jax.ShapeDtypeStruct((n, d), F32),
        scratch_shapes=[pltpu.VMEM((tm, d), BF16), pltpu.VMEM((tm, dff), BF16)],
        compiler_params=_cparams("parallel"),
        name="ffn_half",
    )(*ins)


def _rope_lanes(z, c, s_up, s_dn, shift):
    return z * c + pltpu.roll(z, LANES - shift, 1) * s_up + pltpu.roll(z, shift, 1) * s_dn


def _rope_tables_a(pos, hd, rot):
    half = rot // 2
    lane = np.arange(LANES) % hd
    idx = np.where(lane < rot, lane % half, 0)
    inv = 1.0 / (ROPE_THETA ** (jnp.arange(0, rot, 2, dtype=F32) / rot))
    ang = pos.astype(F32)[:, None] * inv[None, :]
    cos = jnp.cos(ang)[:, idx]
    sin = jnp.sin(ang)[:, idx]
    first = jnp.asarray(lane < half)[None, :]
    second = jnp.asarray((lane >= half) & (lane < rot))[None, :]
    c = jnp.where(first | second, cos, 1.0)
    s_up = jnp.where(first, -sin, 0.0)
    s_dn = jnp.where(second, sin, 0.0)
    return c, s_up, s_dn


def _rope_tables_b(pos, off, dim):
    half = dim // 2
    lane = np.arange(LANES)
    rel = lane - off
    idx = np.where((rel >= 0) & (rel < dim), rel % half, 0)
    inv = 1.0 / (ROPE_THETA ** (jnp.arange(0, dim, 2, dtype=F32) / dim))
    ang = pos.astype(F32)[:, None] * inv[None, :]
    cos = jnp.cos(ang)[:, idx]
    sin = jnp.sin(ang)[:, idx]
    first = jnp.asarray((rel >= 0) & (rel < half))[None, :]
    second = jnp.asarray((rel >= half) & (rel < dim))[None, :]
    c = jnp.where(first | second, cos, 1.0)
    s_up = jnp.where(first, -sin, 0.0)
    s_dn = jnp.where(second, sin, 0.0)
    return c, s_up, s_dn


def _att_proj_body(x_ref, g_ref, win_ref, qng_ref, wuq_ref, kvg_ref, wuk_ref, wuv_ref,
                   ca_ref, ua_ref, da_ref, cb_ref, ub_ref, db_ref,
                   qa_o, kaf_o, kab_o, vaf_o, vab_o, vat_o, qm_o, c_o, kr_o, km_o, vmt_o,
                   *, wa, n_hb, rot_a, rope_b, rope_off, scale_a, scale_b):
    h = _rms(x_ref[...], g_ref[...]).astype(BF16)
    ca, ua, da = ca_ref[...], ua_ref[...], da_ref[...]
    cb, ub, db = cb_ref[...], ub_ref[...], db_ref[...]
    sh_a = rot_a // 2
    sh_b = rope_b // 2

    n_ha = wa // LANES
    tm = x_ref.shape[0]
    zq = _dot(h, win_ref[:, 0:wa])
    zk = _dot(h, win_ref[:, wa:2 * wa])
    for j in range(n_ha):
        sl = slice(j * LANES, (j + 1) * LANES)
        qa_o[:, sl] = (_rope_lanes(zq[:, sl], ca, ua, da, sh_a) * scale_a).astype(BF16)
        kr = _rope_lanes(zk[:, sl], ca, ua, da, sh_a)
        kab_o[:, sl] = kr.astype(BF16)
        kaf_o[pl.ds(j, tm, stride=n_ha), :] = kr
    zv = _dot(h, win_ref[:, 2 * wa:3 * wa])
    for j in range(n_ha):
        vaf_o[pl.ds(j, tm, stride=n_ha), :] = zv[:, j * LANES:(j + 1) * LANES]
    vab_o[...] = zv.astype(BF16)
    vat_o[...] = zv.T.astype(BF16)

    o = 3 * wa
    ql = qng_ref.shape[1]
    cq = _rms(_dot(h, win_ref[:, o:o + ql]), qng_ref[...]).astype(BF16)
    qb = _dot(cq, wuq_ref[...])
    o += ql
    kvl = kvg_ref.shape[1]
    c = _rms(_dot(h, win_ref[:, o:o + kvl]), kvg_ref[...])
    c_o[...] = c
    cbf = c.astype(BF16)
    o += kvl
    kr = _rope_lanes(_dot(h, win_ref[:, o:o + LANES]), cb, ub, db, sh_b)
    kr_o[...] = kr[:, rope_off:rope_off + rope_b]
    kn = _dot(cbf, wuk_ref[...])
    for j in range(n_hb):
        sl = slice(j * LANES, (j + 1) * LANES)
        qm_o[:, sl] = (_rope_lanes(qb[:, sl], cb, ub, db, sh_b) * scale_b).astype(BF16)
        km_o[:, sl] = (kn[:, sl] + kr).astype(BF16)
    vmt_o[...] = _dot(cbf, wuv_ref[...]).T.astype(BF16)


def _att_weights(w_in, w_uq, w_uk, w_uv, wa, q_lora, kv_lora, rope_b, n_hb, nope_b, v_b):
    d = w_in.shape[0]
    o = 3 * wa + q_lora + kv_lora
    kr_cols = jnp.zeros((d, LANES), F32).at[:, nope_b:nope_b + rope_b].set(w_in[:, o:o + rope_b])
    win_p = jnp.concatenate([w_in[:, :o], kr_cols], axis=1).astype(BF16)
    hq = nope_b + rope_b
    wuq_p = jnp.zeros((q_lora, n_hb, LANES), F32).at[:, :, :hq].set(w_uq.reshape(q_lora, n_hb, hq))
    wuq_p = wuq_p.reshape(q_lora, n_hb * LANES).astype(BF16)
    wuk_p = jnp.zeros((kv_lora, n_hb, LANES), F32).at[:, :, :nope_b].set(w_uk)
    wuk_p = wuk_p.reshape(kv_lora, n_hb * LANES).astype(BF16)
    wuv_p = w_uv.reshape(kv_lora, n_hb * v_b).astype(BF16)
    return win_p, wuq_p, wuk_p, wuv_p


def _att_proj(x, g, weights, qng, kvg, tabs_a, tabs_b, tab_index, *, wa, n_hb, rot_a, rope_b, nope_b, v_b,
              scale_a, scale_b):
    n, d = x.shape
    win_p, wuq_p, wuk_p, wuv_p = weights
    tm = _row_tile(n)
    tab_spec = pl.BlockSpec((tm, LANES), lambda i: (tab_index(i), 0))
    kv_lora = kvg.shape[0]
    row_out = lambda w, dt: (_rows(tm, w), jax.ShapeDtypeStruct((n, w), dt))
    t_out = lambda w: (pl.BlockSpec((None, w, tm), lambda i: (i, 0, 0)), jax.ShapeDtypeStruct((n // tm, w, tm), BF16))
    n_ha = wa // LANES
    flat_out = (_rows(tm * n_ha, LANES), jax.ShapeDtypeStruct((n * n_ha, LANES), F32))
    outs = [row_out(wa, BF16), flat_out, row_out(wa, BF16), flat_out, row_out(wa, BF16), t_out(wa),
            row_out(n_hb * LANES, BF16), row_out(kv_lora, F32), row_out(rope_b, F32), row_out(n_hb * LANES, BF16),
            t_out(n_hb * v_b)]
    return pl.pallas_call(
        functools.partial(_att_proj_body, wa=wa, n_hb=n_hb, rot_a=rot_a, rope_b=rope_b, rope_off=nope_b,
                          scale_a=scale_a, scale_b=scale_b),
        grid=(n // tm,),
        in_specs=[_rows(tm, d), _resident((1, d)), _resident(win_p.shape), _resident((1, qng.shape[0])),
                  _resident(wuq_p.shape), _resident((1, kv_lora)), _resident(wuk_p.shape),
                  _resident(wuv_p.shape)] + [tab_spec] * 6,
        out_specs=[spec for spec, _ in outs],
        out_shape=[shape for _, shape in outs],
        compiler_params=_cparams("parallel"),
        name="att_proj",
    )(x, g.reshape(1, d), win_p, qng.reshape(1, -1), wuq_p, kvg.reshape(1, -1), wuk_p, wuv_p,
      *tabs_a, *tabs_b)


def _lam_of(dl, lam_init):
    s01 = jnp.sum(dl[0:1] * dl[1:2], axis=-1, keepdims=True)
    s23 = jnp.sum(dl[2:3] * dl[3:4], axis=-1, keepdims=True)
    return jnp.exp(s01) - jnp.exp(s23) + lam_init


FLASH_TQ = 512
FLASH_TK = 512


def _flash_t_body(dl_ref, g_ref, q_ref, k_ref, vt_ref, o_ref, *, diff, nj, tq, tk, half, lam_init):
    qi = pl.program_id(2)
    qw = LANES if diff else 2 * LANES
    lane = lax.broadcasted_iota(jnp.int32, (tq, LANES), 1)
    qs = []
    for j in range(nj):
        q = q_ref[:, j * qw:(j + 1) * qw]
        if diff:
            qs += [jnp.where(lane < half, q, jnp.zeros_like(q)), jnp.where(lane >= half, q, jnp.zeros_like(q))]
        else:
            qs += [q[:, :LANES], q[:, LANES:]]
    nc = 2 * nj
    vrows = LANES if diff else half
    q0 = qi * tq
    n_full = q0 // tk

    def step(ki, carry, masked):
        ks = pl.multiple_of(ki * tk, tk)
        ss = []
        for c in range(nc):
            j, a = divmod(c, 2)
            k0 = j * qw + (0 if diff else a * LANES)
            s = _dot_nt(k_ref[pl.ds(ks, tk), k0:k0 + LANES], qs[c])
            if masked:
                key = lax.broadcasted_iota(jnp.int32, (tk, tq), 0) + ks
                qry = lax.broadcasted_iota(jnp.int32, (tk, tq), 1) + q0
                s = jnp.where(key <= qry, s, NEG_INF)
            ss.append(s)
        out = []
        for c in range(nc):
            j, a = divmod(c, 2)
            m_old, l_old, acc = carry[c]
            m_new = jnp.maximum(m_old, jnp.max(ss[c], axis=0, keepdims=True))
            corr = jnp.exp2(m_old - m_new)
            p = jnp.exp2(ss[c] - m_new)
            l_new = l_old * corr + jnp.sum(p, axis=0, keepdims=True)
            r0 = j * LANES + (0 if diff else a * half)
            acc = acc * corr + _dot(vt_ref[ki, r0:r0 + vrows, :], p.astype(BF16))
            out.append((m_new, l_new, acc))
        return tuple(out)

    init = tuple((jnp.full((1, tq), NEG_INF, F32), jnp.zeros((1, tq), F32), jnp.zeros((vrows, tq), F32))
                 for _ in range(nc))
    carry = lax.fori_loop(0, n_full, lambda ki, cr: step(ki, cr, False), init)
    carry = step(n_full, carry, True)
    if diff:
        lam = _lam_of(dl_ref[...], lam_init)
    for j in range(nj):
        o0 = carry[2 * j][2] / carry[2 * j][1]
        o1 = carry[2 * j + 1][2] / carry[2 * j + 1][1]
        if diff:
            d = o0 - lam * o1
            ot = d * lax.rsqrt(jnp.mean(d * d, axis=0, keepdims=True) + RMS_EPS) * g_ref[...] * (1.0 - lam_init)
        else:
            ot = jnp.concatenate([o0, o1], axis=0)
        o_ref[:, j * LANES:(j + 1) * LANES] = ot.T.astype(o_ref.dtype)


def _flash_t(q, k, vt, dl, g_col, *, diff, half, lam_init):
    b, s, _ = q.shape
    tk = vt.shape[2]
    jt = vt.shape[1] // LANES
    nj = math.gcd(jt, 4)
    qw = LANES if diff else 2 * LANES
    tq = min(FLASH_TQ, tk)
    assert s % tk == 0 and tk % tq == 0
    return pl.pallas_call(
        functools.partial(_flash_t_body, diff=diff, nj=nj, tq=tq, tk=tk, half=half, lam_init=lam_init),
        grid=(b, jt // nj, s // tq),
        in_specs=[pl.BlockSpec(dl.shape, lambda bi, ji, qi: (0, 0)),
                  pl.BlockSpec(g_col.shape, lambda bi, ji, qi: (0, 0)),
                  pl.BlockSpec((None, tq, nj * qw), lambda bi, ji, qi: (bi, qi, ji)),
                  pl.BlockSpec((None, s, nj * qw), lambda bi, ji, qi: (bi, 0, ji), pipeline_mode=pl.Buffered(1)),
                  pl.BlockSpec((s // tk, nj * LANES, tk), lambda bi, ji, qi: (bi, ji, 0),
                               pipeline_mode=pl.Buffered(1))],
        out_specs=pl.BlockSpec((None, tq, nj * LANES), lambda bi, ji, qi: (bi, qi, ji)),
        out_shape=jax.ShapeDtypeStruct((b, s, jt * LANES), BF16),
        compiler_params=_cparams("parallel", "parallel", "arbitrary"),
        name="flash_diff" if diff else "flash_mla",
    )(dl, g_col, q, k, vt)


def _online(m_sc, l_sc, acc_sc, s, v):
    m_old = m_sc[...]
    m_new = jnp.maximum(m_old, jnp.max(s, axis=-1, keepdims=True))
    corr = jnp.exp2(m_old - m_new)
    p = jnp.exp2(s - m_new)
    l_sc[...] = l_sc[...] * corr + jnp.sum(p, axis=-1, keepdims=True)
    acc_sc[...] = acc_sc[...] * corr + _dot(p.astype(BF16), v)
    m_sc[...] = m_new


def _page_pipeline(pt_ref, caches, bufs, sems, layer, npg, two_threads=False):
    ns = pl.num_programs(1)
    total = pl.num_programs(0) * ns
    g = pl.program_id(0) * ns + pl.program_id(1)
    slot = lax.rem(g, 2)

    def copy(c, page_id, slot_, i):
        return pltpu.make_async_copy(caches[c].at[layer, page_id], bufs[c].at[slot_, i], sems.at[slot_, i, c])

    def start(group, slot_):
        b = group // ns
        first = lax.rem(group, ns) * npg
        for i in range(npg):
            page_id = pt_ref[b, first + i]
            for c in range(len(caches)):
                copy(c, page_id, slot_, i).start(priority=(i + c) % 2 if two_threads else 0)

    @pl.when(g == 0)
    def _():
        start(g, slot)

    @pl.when(g + 1 < total)
    def _():
        start(g + 1, 1 - slot)

    for i in range(npg):
        for c in range(len(caches)):
            copy(c, 0, slot, i).wait()
    return slot


def _pages_per_step(n_pages, cap):
    return next(gp for gp in range(min(cap, n_pages), 0, -1) if n_pages % gp == 0)


def _diff_dec_body(pt_ref, dl_ref, g_ref, wq_ref, kn_ref, vn_ref, ck_hbm, cv_hbm, o_ref,
                   kbuf, vbuf, sems, m_sc, l_sc, acc_sc, *, layer, npg, n_h, t, lam_init):
    slot = _page_pipeline(pt_ref, (ck_hbm, cv_hbm), (kbuf, vbuf), sems, layer, npg)
    st = pl.program_id(1)
    page = kbuf.shape[2] // n_h

    @pl.when(st == 0)
    def _():
        m_sc[...] = jnp.full_like(m_sc, NEG_INF)
        l_sc[...] = jnp.zeros_like(l_sc)
        acc_sc[...] = jnp.zeros_like(acc_sc)

    def head_rows(buf, h):
        return jnp.concatenate([buf[slot, i, pl.ds(h, page, stride=n_h), :].astype(BF16) for i in range(npg)], axis=0)

    m_old, l_old, acc_old = m_sc[...], l_sc[...], acc_sc[...]
    ss = [_dot_nt(wq_ref[h], head_rows(kbuf, h)) for h in range(n_h)]
    m_new = [jnp.maximum(m_old[h], jnp.max(ss[h], axis=-1, keepdims=True)) for h in range(n_h)]
    ps = [jnp.exp2(ss[h] - m_new[h]) for h in range(n_h)]
    pv = [_dot(ps[h].astype(BF16), head_rows(vbuf, h)) for h in range(n_h)]
    for h in range(n_h):
        corr = jnp.exp2(m_old[h] - m_new[h])
        m_sc[h] = m_new[h]
        l_sc[h] = l_old[h] * corr + jnp.sum(ps[h], axis=-1, keepdims=True)
        acc_sc[h] = acc_old[h] * corr + pv[h]

    @pl.when(st == pl.num_programs(1) - 1)
    def _():
        for h in range(n_h):
            kn = kn_ref[h]
            row = lax.broadcasted_iota(jnp.int32, (2 * t, kn.shape[0]), 0)
            col = lax.broadcasted_iota(jnp.int32, (2 * t, kn.shape[0]), 1)
            s = jnp.where(col <= lax.rem(row, t), _dot_nt(wq_ref[h], kn), NEG_INF)
            _online(m_sc.at[h], l_sc.at[h], acc_sc.at[h], s, vn_ref[h])
        o = acc_sc[...] / l_sc[...]
        lam = _lam_of(dl_ref[...], lam_init)
        o_ref[...] = _rms(o[:, :t] - lam * o[:, t:], g_ref[...]) * (1.0 - lam_init)


def _diff_dec(page_table, dl, g, wq, kn, vn, cache_k, cache_v, layer, *, t, lam_init):
    nb, n_pages = page_table.shape
    npg = _pages_per_step(n_pages, 16)
    n_h = wq.shape[1]
    prow = cache_k.shape[2]
    per_b = lambda shape: pl.BlockSpec((None,) + shape, lambda b, s, pt: (b,) + (0,) * len(shape))
    const = lambda shape: pl.BlockSpec(shape, lambda b, s, pt: (0,) * len(shape))
    hbm = pl.BlockSpec(memory_space=pl.ANY)
    grid_spec = pltpu.PrefetchScalarGridSpec(
        num_scalar_prefetch=1,
        grid=(nb, n_pages // npg),
        in_specs=[const(dl.shape), const(g.shape), per_b(wq.shape[1:]), per_b(kn.shape[1:]), per_b(vn.shape[1:]),
                  hbm, hbm],
        out_specs=per_b((n_h, t, LANES)),
        scratch_shapes=[pltpu.VMEM((2, npg, prow, LANES), F32), pltpu.VMEM((2, npg, prow, LANES), F32),
                        pltpu.SemaphoreType.DMA((2, npg, 2)),
                        pltpu.VMEM((n_h, 2 * t, 1), F32), pltpu.VMEM((n_h, 2 * t, 1), F32),
                        pltpu.VMEM((n_h, 2 * t, LANES), F32)],
    )
    return pl.pallas_call(
        functools.partial(_diff_dec_body, layer=layer, npg=npg, n_h=n_h, t=t, lam_init=lam_init),
        grid_spec=grid_spec,
        out_shape=jax.ShapeDtypeStruct((nb, n_h, t, LANES), F32),
        compiler_params=_cparams("arbitrary", "arbitrary"),
        name="diff_decode",
    )(page_table, dl, g, wq, kn, vn, cache_k, cache_v)


def _mla_dec_body(pt_ref, ql_ref, qr_ref, cn_ref, rn_ref, lat_hbm, rope_hbm, o_ref,
                  lbuf, rbuf, sems, m_sc, l_sc, acc_sc, *, layer, npg, t):
    slot = _page_pipeline(pt_ref, (lat_hbm, rope_hbm), (lbuf, rbuf), sems, layer, npg, two_threads=True)
    st = pl.program_id(1)
    nrow = ql_ref.shape[0]

    @pl.when(st == 0)
    def _():
        m_sc[...] = jnp.full_like(m_sc, NEG_INF)
        l_sc[...] = jnp.zeros_like(l_sc)
        acc_sc[...] = jnp.zeros_like(acc_sc)

    ql = ql_ref[...]
    qr = qr_ref[...]
    lat = jnp.concatenate([lbuf[slot, i].astype(BF16) for i in range(npg)], axis=0)
    rope_t = jnp.concatenate([rbuf[slot, i].astype(BF16) for i in range(npg)], axis=1)
    _online(m_sc, l_sc, acc_sc, _dot_nt(ql, lat) + _dot(qr, rope_t), lat)

    @pl.when(st == pl.num_programs(1) - 1)
    def _():
        cn = cn_ref[...]
        ncol = cn.shape[0]
        row = lax.broadcasted_iota(jnp.int32, (nrow, ncol), 0)
        col = lax.broadcasted_iota(jnp.int32, (nrow, ncol), 1)
        s = jnp.where(col <= (row % t), _dot_nt(ql, cn) + _dot_nt(qr, rn_ref[...]), NEG_INF)
        _online(m_sc, l_sc, acc_sc, s, cn)
        o_ref[...] = acc_sc[...] / l_sc[...]


def _mla_dec(page_table, ql, qr, cn, rn, cache_lat, cache_rope_t, layer, *, t):
    nb, n_pages = page_table.shape
    npg = _pages_per_step(n_pages, 64)
    nrow, r = ql.shape[1:]
    e = qr.shape[2]
    page = cache_lat.shape[2]
    per_b = lambda shape: pl.BlockSpec((None,) + shape, lambda b, s, pt: (b, 0, 0))
    hbm = pl.BlockSpec(memory_space=pl.ANY)
    grid_spec = pltpu.PrefetchScalarGridSpec(
        num_scalar_prefetch=1,
        grid=(nb, n_pages // npg),
        in_specs=[per_b((nrow, r)), per_b((nrow, e)), per_b(cn.shape[1:]), per_b(rn.shape[1:]), hbm, hbm],
        out_specs=per_b((nrow, r)),
        scratch_shapes=[pltpu.VMEM((2, npg, page, r), F32), pltpu.VMEM((2, npg, e, page), F32),
                        pltpu.SemaphoreType.DMA((2, npg, 2)),
                        pltpu.VMEM((nrow, 1), F32), pltpu.VMEM((nrow, 1), F32), pltpu.VMEM((nrow, r), F32)],
    )
    return pl.pallas_call(
        functools.partial(_mla_dec_body, layer=layer, npg=npg, t=t),
        grid_spec=grid_spec,
        out_shape=jax.ShapeDtypeStruct((nb, nrow, r), F32),
        compiler_params=_cparams("arbitrary", "arbitrary"),
        name="mla_decode",
    )(page_table, ql, qr, cn, rn, cache_lat, cache_rope_t)


def _head_mm_body(x_ref, w_ref, o_ref, *, n_h, wi, wo):
    for h in range(n_h):
        o_ref[:, h * wo:(h + 1) * wo] = _dot(x_ref[:, h * wi:(h + 1) * wi], w_ref[h]).astype(o_ref.dtype)


def _head_mm(x, w, out_dtype):
    n = x.shape[0]
    n_h, wi, wo = w.shape
    return pl.pallas_call(
        functools.partial(_head_mm_body, n_h=n_h, wi=wi, wo=wo),
        out_shape=jax.ShapeDtypeStruct((n, n_h * wo), out_dtype),
        compiler_params=pltpu.CompilerParams(vmem_limit_bytes=VMEM_LIMIT),
        name="head_matmul",
    )(x, w)


def _att_merge_body(x_ref, oa_ref, ob_ref, wa_ref, wb_ref, o_ref):
    o_ref[...] = x_ref[...] + _dot(oa_ref[...], wa_ref[...]) + _dot(ob_ref[...], wb_ref[...])


def _att_merge(x, oa, ob, wa, wb):
    n, d = x.shape
    tm = _row_tile(n)
    return pl.pallas_call(
        _att_merge_body,
        grid=(n // tm,),
        in_specs=[_rows(tm, d), _rows(tm, oa.shape[1]), _rows(tm, ob.shape[1]), _resident(wa.shape),
                  _resident(wb.shape)],
        out_specs=_rows(tm, d),
        out_shape=jax.ShapeDtypeStruct((n, d), F32),
        compiler_params=_cparams("parallel"),
        name="att_merge",
    )(x, oa, ob, wa, wb)


def _ret_merge_body(x_ref, y_ref, g_ref, w_ref, o_ref):
    g = g_ref[...]
    z = (y_ref[...].astype(F32) * (g * jax.nn.sigmoid(g))).astype(BF16)
    o_ref[...] = x_ref[...] + _dot(z, w_ref[...])


def _ret_merge(x, y, g, w):
    n, d = x.shape
    tm = _row_tile(n)
    return pl.pallas_call(
        _ret_merge_body,
        grid=(n // tm,),
        in_specs=[_rows(tm, d), _rows(tm, y.shape[1]), _rows(tm, g.shape[1]), _resident(w.shape)],
        out_specs=_rows(tm, d),
        out_shape=jax.ShapeDtypeStruct((n, d), F32),
        compiler_params=_cparams("parallel"),
        name="ret_merge",
    )(x, y, g, w)


def _ret_proj_body(x_ref, g_ref, w_ref, cos_ref, sin_ref, q_o, k_o, v_o, g_o, *, n_h, dk, dv, k_scale):
    h = _rms(x_ref[...], g_ref[...]).astype(BF16)
    cos, sin = cos_ref[...], sin_ref[...]
    half = dk // 2
    wq = n_h * dk
    for part, (out, scale) in enumerate(((q_o, 1.0), (k_o, k_scale))):
        z = _dot(h, w_ref[:, part * wq:(part + 1) * wq])
        for j in range(n_h):
            for c0 in range(0, half, LANES):
                a = slice(j * dk + c0, j * dk + c0 + LANES)
                b = slice(j * dk + half + c0, j * dk + half + c0 + LANES)
                cs, sn = cos[:, c0:c0 + LANES], sin[:, c0:c0 + LANES]
                z1, z2 = z[:, a], z[:, b]
                out[:, a] = ((z1 * cs - z2 * sn) * scale).astype(BF16)
                out[:, b] = ((z1 * sn + z2 * cs) * scale).astype(BF16)
    wv = n_h * dv
    v_o[...] = _dot(h, w_ref[:, 2 * wq:2 * wq + wv]).astype(BF16)
    g_o[...] = _dot(h, w_ref[:, 2 * wq + wv:2 * wq + 2 * wv])


def _ret_proj(x, g, w, cos, sin, tab_index, *, n_h, dk, dv):
    n, d = x.shape
    tm = _row_tile(n)
    half = dk // 2
    tab_spec = pl.BlockSpec((tm, half), lambda i: (tab_index(i), 0))
    out_w = [(n_h * dk, BF16), (n_h * dk, BF16), (n_h * dv, BF16), (n_h * dv, F32)]
    return pl.pallas_call(
        functools.partial(_ret_proj_body, n_h=n_h, dk=dk, dv=dv, k_scale=dk ** -0.5),
        grid=(n // tm,),
        in_specs=[_rows(tm, d), _resident((1, d)), _resident(w.shape), tab_spec, tab_spec],
        out_specs=[_rows(tm, wd) for wd, _ in out_w],
        out_shape=[jax.ShapeDtypeStruct((n, wd), dt) for wd, dt in out_w],
        compiler_params=_cparams("parallel"),
        name="ret_proj",
    )(x, g.reshape(1, d), w, cos, sin)


def _ret_body(*refs, has_s0):
    if has_s0:
        q_ref, k_ref, v_ref, in_ref, qd_ref, kd_ref, cd_ref, gn_ref, s0_ref, o_ref, st_ref = refs
    else:
        q_ref, k_ref, v_ref, in_ref, qd_ref, kd_ref, cd_ref, gn_ref, o_ref, st_ref = refs
    c = pl.program_id(1)
    n_h, dk, dv = st_ref.shape

    @pl.when(c == 0)
    def _():
        st_ref[...] = s0_ref[...] if has_s0 else jnp.zeros_like(st_ref)

    sts = [st_ref[h] for h in range(n_h)]
    new = []
    for h in range(n_h):
        q, k = q_ref[:, h * dk:(h + 1) * dk], k_ref[:, h * dk:(h + 1) * dk]
        v = v_ref[:, h * dv:(h + 1) * dv]
        att = (_dot_nt(q, k) * in_ref[h]).astype(BF16)
        o = _dot(att, v) + _dot(q, sts[h].astype(BF16)) * qd_ref[h]
        kd = (k.astype(F32) * kd_ref[h]).astype(BF16)
        new.append(sts[h] * cd_ref[h] + _dot_tn(kd, v))
        o_ref[:, h * dv:(h + 1) * dv] = _rms(o, gn_ref[...]).astype(o_ref.dtype)
    for h in range(n_h):
        st_ref[h] = new[h]


def _ret_tables(n_h, c, rows):
    log_g = jnp.log1p(-jnp.exp2(-5.0 - jnp.arange(n_h, dtype=F32)))
    idx = jnp.arange(c, dtype=F32)
    dif = idx[:, None] - idx[None, :]
    intra = jnp.where(dif >= 0, jnp.exp(jnp.maximum(dif, 0.0)[None] * log_g[:, None, None]), 0.0)
    q_dec = jnp.exp((idx + 1.0)[None] * log_g[:, None])[..., None]
    k_dec = jnp.exp((c - 1.0 - idx)[None] * log_g[:, None])[..., None]
    c_dec = jnp.exp(c * log_g)[:, None, None]
    p = rows - c
    return (jnp.pad(intra, ((0, 0), (0, p), (0, p))), jnp.pad(q_dec, ((0, 0), (0, p), (0, 0))),
            jnp.pad(k_dec, ((0, 0), (0, p), (0, 0))), c_dec)


def _retention(q, k, v, tables, gn, s0, *, n_h, c):
    b, t, _ = q.shape
    dk = q.shape[2] // n_h
    dv = v.shape[2] // n_h
    intra, q_dec, k_dec, c_dec = tables
    blk = lambda w: pl.BlockSpec((None, c, n_h * w), lambda bi, ci: (bi, ci, 0))
    table = lambda shape: pl.BlockSpec(shape, lambda bi, ci: (0,) * len(shape))
    st_spec = pl.BlockSpec((None, n_h, dk, dv), lambda bi, ci: (bi, 0, 0, 0))
    ins = [q, k, v, intra, q_dec, k_dec, c_dec, gn.reshape(1, dv)]
    specs = [blk(dk), blk(dk), blk(dv), table(intra.shape), table(q_dec.shape), table(k_dec.shape),
             table(c_dec.shape), table((1, dv))]
    if s0 is not None:
        ins.append(s0)
        specs.append(st_spec)
    return pl.pallas_call(
        functools.partial(_ret_body, has_s0=s0 is not None),
        grid=(b, t // c),
        in_specs=specs,
        out_specs=[blk(dv), st_spec],
        out_shape=[jax.ShapeDtypeStruct((b, t, n_h * dv), BF16), jax.ShapeDtypeStruct((b, n_h, dk, dv), F32)],
        compiler_params=_cparams("parallel", "arbitrary"),
        name="retention",
    )(*ins)


def kernel(x_prompt, x_sample, cache_a_k, cache_a_v, cache_b_lat, cache_b_rope, state_ret, page_table,
           norm_g, ffn_w_gate, ffn_w_up, ffn_w_down, att_w_in, diff_lambda, diff_subln_g,
           mla_q_norm_g, mla_w_uq, mla_kv_norm_g, mla_w_uk, mla_w_uv, att_w_out,
           ret_w_in, ret_gn_g, ret_w_out, final_norm_g):
    bp, sp, d = x_prompt.shape
    bs, ts, _ = x_sample.shape
    depth = norm_g.shape[0]
    n_layers_att, pool, page, h_a, hd2 = cache_a_k.shape
    hd_a = hd2 // 2
    rot_a = hd_a // 4
    wa = h_a * hd2
    kv_lora, h_b, nope_b = mla_w_uk.shape[1:]
    v_b = mla_w_uv.shape[3]
    rope_b = cache_b_rope.shape[3]
    q_lora = mla_q_norm_g.shape[1]
    h_r, dk_r, dv_r = state_ret.shape[2:]
    past = page_table.shape[1] * page
    n_p, n_s = bp * sp, bs * ts

    pos_p = jnp.arange(sp)
    pos_s = past + jnp.arange(ts)
    tm_p, tm_s = _row_tile(n_p), _row_tile(n_s)
    pos_s_rows = jnp.tile(pos_s, tm_s // ts)
    idx_p = lambda i: i % (sp // tm_p)
    idx_s = lambda i: 0

    xp = x_prompt.reshape(n_p, d)
    xs = x_sample.reshape(n_s, d)
    outs_p = {k: [] for k in ("ak", "av", "lat", "kr", "ret")}
    outs_s = {k: [] for k in ("ak", "av", "lat", "kr", "ret")}

    cache_k4 = cache_a_k.reshape(n_layers_att, pool, page * h_a, hd2)
    cache_v4 = cache_a_v.reshape(n_layers_att, pool, page * h_a, hd2)

    w_ffn = (ffn_w_gate.astype(BF16), ffn_w_up.astype(BF16), ffn_w_down.astype(BF16))
    for layer in range(depth):
        j = layer // 2
        xp = _ffn(xp, norm_g[layer, 0], w_ffn, layer, 0)
        xs = _ffn(xs, norm_g[layer, 0], w_ffn, layer, 0)
        if layer % 2 == 0:
            lam_init = 0.8 - 0.6 * math.exp(-0.3 * layer)
            scale_a = hd_a ** -0.5 * LOG2E
            scale_b = (nope_b + rope_b) ** -0.5 * LOG2E
            weights = _att_weights(att_w_in[j], mla_w_uq[j], mla_w_uk[j], mla_w_uv[j], wa, q_lora, kv_lora,
                                   rope_b, h_b, nope_b, v_b)
            proj = functools.partial(_att_proj, weights=weights, qng=mla_q_norm_g[j], kvg=mla_kv_norm_g[j],
                                     wa=wa, n_hb=h_b, rot_a=rot_a, rope_b=rope_b, nope_b=nope_b, v_b=v_b,
                                     scale_a=scale_a, scale_b=scale_b)
            dl = diff_lambda[j]
            sg = diff_subln_g[j].reshape(1, hd2)
            w_oa = att_w_out[j, :wa].astype(BF16)
            w_ob = att_w_out[j, wa:].astype(BF16)

            qa, kaf, kab, vaf, _, vat, qm, c, kr, km, vmt = proj(
                xp, norm_g[layer, 1], tabs_a=_rope_tables_a(pos_p, hd_a, rot_a),
                tabs_b=_rope_tables_b(pos_p, nope_b, rope_b), tab_index=idx_p)
            r3 = lambda z: z.reshape(bp, sp, z.shape[1])
            sg_col = diff_subln_g[j].reshape(hd2, 1)
            o_a = _flash_t(r3(qa), r3(kab), vat, dl, sg_col, diff=True, half=hd_a, lam_init=lam_init)
            o_b = _flash_t(r3(qm), r3(km), vmt, dl, sg_col, diff=False, half=v_b, lam_init=lam_init)
            xp = _att_merge(xp, o_a.reshape(n_p, wa), o_b.reshape(n_p, h_b * v_b), w_oa, w_ob)
            outs_p["ak"].append(kaf.reshape(bp, sp, h_a, hd2))
            outs_p["av"].append(vaf.reshape(bp, sp, h_a, hd2))
            outs_p["lat"].append(c.reshape(bp, sp, kv_lora))
            outs_p["kr"].append(kr.reshape(bp, sp, rope_b))

            qa, kaf, kab, vaf, vab, _, qm, c, kr, _, _ = proj(
                xs, norm_g[layer, 1], tabs_a=_rope_tables_a(pos_s_rows, hd_a, rot_a),
                tabs_b=_rope_tables_b(pos_s_rows, nope_b, rope_b), tab_index=idx_s)
            q5 = qa.reshape(bs, ts, h_a, 2, hd_a).transpose(0, 2, 3, 1, 4)
            eye = jnp.eye(2, dtype=BF16)[None, None, :, None, :, None]
            wq = (q5[:, :, :, :, None, :] * eye).reshape(bs, h_a, 2 * ts, hd2)
            new_keys = -(-ts // 16) * 16
            pad_new = lambda z: jnp.pad(z.reshape(bs, ts, h_a, hd2).transpose(0, 2, 1, 3),
                                        ((0, 0), (0, 0), (0, new_keys - ts), (0, 0)))
            o_a = _diff_dec(page_table, dl, sg, wq, pad_new(kab), pad_new(vab), cache_k4, cache_v4, j,
                            t=ts, lam_init=lam_init)
            o_a = o_a.transpose(0, 2, 1, 3).reshape(n_s, wa).astype(BF16)

            w_abs = jnp.zeros((h_b, LANES, kv_lora), F32).at[:, :nope_b].set(mla_w_uk[j].transpose(1, 2, 0))
            q_lat = _head_mm(qm, w_abs.astype(BF16), BF16)
            q_lat = q_lat.reshape(bs, ts, h_b, kv_lora).transpose(0, 2, 1, 3).reshape(bs, h_b * ts, kv_lora)
            q_r = qm.reshape(bs, ts, h_b, LANES)[..., nope_b:nope_b + rope_b]
            q_r = q_r.transpose(0, 2, 1, 3).reshape(bs, h_b * ts, rope_b)
            new_keys = -(-ts // 16) * 16
            pad_keys = lambda z: jnp.pad(z.reshape(bs, ts, -1).astype(BF16), ((0, 0), (0, new_keys - ts), (0, 0)))
            o_lat = _mla_dec(page_table, q_lat, q_r, pad_keys(c), pad_keys(kr), cache_b_lat,
                             jnp.swapaxes(cache_b_rope, 2, 3), j, t=ts)
            o_lat = o_lat.reshape(bs, h_b, ts, kv_lora).transpose(0, 2, 1, 3).reshape(n_s, h_b * kv_lora)
            o_b = _head_mm(o_lat.astype(BF16), mla_w_uv[j].transpose(1, 0, 2).astype(BF16), BF16)
            xs = _att_merge(xs, o_a, o_b, w_oa, w_ob)
            outs_s["ak"].append(kaf.reshape(bs, ts, h_a, hd2))
            outs_s["av"].append(vaf.reshape(bs, ts, h_a, hd2))
            outs_s["lat"].append(c.reshape(bs, ts, kv_lora))
            outs_s["kr"].append(kr.reshape(bs, ts, rope_b))
        else:
            w_in = ret_w_in[j].astype(BF16)
            w_out = ret_w_out[j].astype(BF16)
            inv = 1.0 / (RET_THETA ** (jnp.arange(0, dk_r, 2, dtype=F32) / dk_r))

            def tables(pos):
                ang = pos.astype(F32)[:, None] * inv[None, :]
                return jnp.cos(ang), jnp.sin(ang)

            cos, sin = tables(pos_p)
            q, k, v, g = _ret_proj(xp, norm_g[layer, 1], w_in, cos, sin, idx_p, n_h=h_r, dk=dk_r, dv=dv_r)
            cp = RET_CHUNK if sp % RET_CHUNK == 0 else sp
            r3 = lambda z: z.reshape(bp, sp, z.shape[1])
            o, s_fin = _retention(r3(q), r3(k), r3(v), _ret_tables(h_r, cp, cp), ret_gn_g[j], None, n_h=h_r, c=cp)
            xp = _ret_merge(xp, o.reshape(n_p, h_r * dv_r), g, w_out)
            outs_p["ret"].append(s_fin)

            cos, sin = tables(pos_s_rows)
            q, k, v, g = _ret_proj(xs, norm_g[layer, 1], w_in, cos, sin, idx_s, n_h=h_r, dk=dk_r, dv=dv_r)
            cs = RET_CHUNK if ts % RET_CHUNK == 0 else ts
            rows = -(-cs // 16) * 16
            tpad = (ts // cs) * rows
            padt = lambda z: jnp.pad(z.reshape(bs, ts // cs, cs, z.shape[1]),
                                     ((0, 0), (0, 0), (0, rows - cs), (0, 0))).reshape(bs, tpad, z.shape[1])
            o, s_fin = _retention(padt(q), padt(k), padt(v), _ret_tables(h_r, cs, rows), ret_gn_g[j],
                                  state_ret[j], n_h=h_r, c=rows)
            o = o.reshape(bs, ts // cs, rows, h_r * dv_r)[:, :, :cs].reshape(n_s, h_r * dv_r)
            xs = _ret_merge(xs, o, g, w_out)
            outs_s["ret"].append(s_fin)
        fin = final_norm_g if layer == depth - 1 else None
        xp = _ffn(xp, norm_g[layer, 2], w_ffn, layer, 1, fin)
        xs = _ffn(xs, norm_g[layer, 2], w_ffn, layer, 1, fin)

    return (xp.reshape(bp, sp, d), xs.reshape(bs, ts, d),
            jnp.stack(outs_p["ak"]), jnp.stack(outs_p["av"]), jnp.stack(outs_p["lat"]), jnp.stack(outs_p["kr"]),
            jnp.stack(outs_p["ret"]),
            jnp.stack(outs_s["ak"]), jnp.stack(outs_s["av"]), jnp.stack(outs_s["lat"]), jnp.stack(outs_s["kr"]),
            jnp.stack(outs_s["ret"]))
```
